```python
import math
import jax, jax.numpy as jnp
from jax import lax
import numpy as np

D_MODEL = 1024
BATCH = 32
SEQ = 2048
DEPTH = 4
DEC_BATCH = 8
DEC_SEQ = 4096
PAST_LEN = 128

N_META = 16
SSD_HEADS = 16
SSD_HEAD_DIM = 64
SSD_INNER = SSD_HEADS * SSD_HEAD_DIM
SSD_GROUPS = 4
SSD_STATE = 128
SSD_CONV = 4
CHUNK = 128
XBC_DIM = SSD_INNER + 2 * SSD_GROUPS * SSD_STATE
LRU_WIDTH = 1024
LRU_HEADS = 16
LRU_BLOCK = LRU_WIDTH // LRU_HEADS
LRU_CONV = 4
LRU_C = 8.0
D_FF = 3 * D_MODEL
FFN_CONV = 3
S1 = SSD_INNER
S2 = S1 + XBC_DIM
S3 = S2 + 2 * SSD_HEADS
S4 = S3 + LRU_WIDTH
S5 = S4 + LRU_WIDTH
IN_DIM = S5 + 2 * D_MODEL
ALPHA = (2 * DEPTH) ** 0.25
BETA = (8 * DEPTH) ** -0.25
LN_EPS = 1e-5
RMS_EPS = 1e-6

kernel_name = "hybrid_bidir_ssd_rglru_encoder"


def layer_norm(x, g, b):
    xf = x.astype(jnp.float32)
    mu = jnp.mean(xf, -1, keepdims=True)
    var = jnp.mean(jnp.square(xf - mu), -1, keepdims=True)
    return ((xf - mu) * lax.rsqrt(var + LN_EPS) * g + b).astype(x.dtype)


def dwconv(x, w, b, pad_left):
    k_w = w.shape[0]
    t_len = x.shape[1]
    xp = jnp.pad(x, ((0, 0), (pad_left, k_w - 1 - pad_left), (0, 0)))
    return sum(xp[:, k:k + t_len] * w[k] for k in range(k_w)) + b


def segsum(a):
    t_len = a.shape[-1]
    cs = jnp.cumsum(a, -1)
    diff = cs[..., :, None] - cs[..., None, :]
    mask = jnp.tril(jnp.ones((t_len, t_len), bool))
    return jnp.where(mask, diff, -jnp.inf)


def ssd_chunked(x, dt, A, B, C):
    b, l, h, p = x.shape
    g, n = B.shape[-2:]
    r = h // g
    c = l // CHUNK
    xdt = (x * dt[..., None]).reshape(b, c, CHUNK, g, r, p)
    Bc = B.reshape(b, c, CHUNK, g, n)
    Cc = C.reshape(b, c, CHUNK, g, n)
    a = (dt * A).reshape(b, c, CHUNK, g, r).transpose(0, 3, 4, 1, 2)
    a_cs = jnp.cumsum(a, -1)
    Lmat = jnp.exp(segsum(a))
    cb = jnp.einsum("bclgn,bcsgn->bgcls", Cc, Bc)
    scores = cb[:, :, None] * Lmat
    y_diag = jnp.einsum("bgrcls,bcsgrp->bclgrp", scores, xdt)
    decay_states = jnp.exp(a_cs[..., -1:] - a_cs).transpose(0, 3, 4, 1, 2)
    states = jnp.einsum("bclgn,bclgrp->bcgrpn", Bc, xdt * decay_states[..., None])
    chunk_a = jnp.pad(a_cs[..., -1], ((0, 0), (0, 0), (0, 0), (1, 0)))
    decay_chunk = jnp.exp(segsum(chunk_a))
    states0 = jnp.pad(states, ((0, 0), (1, 0), (0, 0), (0, 0), (0, 0), (0, 0)))
    states_all = jnp.einsum("bgrzc,bcgrpn->bzgrpn", decay_chunk, states0)
    prev = states_all[:, :-1]
    y_off = jnp.einsum("bclgn,bcgrpn->bclgrp", Cc, prev) * \
        jnp.exp(a_cs).transpose(0, 3, 4, 1, 2)[..., None]
    return (y_diag + y_off).reshape(b, l, h, p)


def ssd_mixer(z, xbc_raw, dt_raw, conv_w, conv_b, dt_bias, a_log, d_skip, norm_g):
    b, l, _ = z.shape
    xbc = jax.nn.silu(dwconv(xbc_raw, conv_w, conv_b, SSD_CONV // 2))
    xs, Bm, Cm = jnp.split(xbc, [SSD_INNER, SSD_INNER + SSD_GROUPS * SSD_STATE], -1)
    xs = xs.reshape(b, l, SSD_HEADS, SSD_HEAD_DIM)
    Bm = Bm.reshape(b, l, SSD_GROUPS, SSD_STATE)
    Cm = Cm.reshape(b, l, SSD_GROUPS, SSD_STATE)
    dt = jax.nn.softplus(dt_raw.astype(jnp.float32).reshape(b, l, 2, SSD_HEADS)
                         + dt_bias.astype(jnp.float32))
    A = -jnp.exp(a_log.astype(jnp.float32))
    pad = CHUNK - N_META
    def padt(t):
        return jnp.pad(t, ((0, 0), (pad, 0)) + ((0, 0),) * (t.ndim - 2))
    xs_p, B_p, C_p, dt_p = padt(xs), padt(Bm), padt(Cm), padt(dt)
    def flip(t):
        return jnp.flip(t, 1)
    y_f = ssd_chunked(xs_p, dt_p[:, :, 0], A[0], B_p, C_p)
    y_b = flip(ssd_chunked(flip(xs_p), flip(dt_p[:, :, 1]), A[1], flip(B_p), flip(C_p)))
    y = (y_f + y_b)[:, pad:] + xs * d_skip[:, None]
    y = y.reshape(b, l, SSD_INNER) * jax.nn.silu(z)
    yf = y.astype(jnp.float32)
    yf = yf * lax.rsqrt(jnp.mean(jnp.square(yf), -1, keepdims=True) + RMS_EPS) * norm_g
    return yf.astype(z.dtype)


def rglru_mixer(gate_raw, x_raw, conv_w, conv_b, w_rg, b_rg, w_ig, b_ig, lam):
    b, l, _ = x_raw.shape
    xc = dwconv(x_raw, conv_w, conv_b, LRU_CONV // 2)
    xh = xc.reshape(b, l, LRU_HEADS, LRU_BLOCK)
    r = jax.nn.sigmoid(jnp.einsum("blhi,dhij->dblhj", xh, w_rg).reshape(2, b, l, LRU_WIDTH)
                       + b_rg[:, None, None])
    i = jax.nn.sigmoid(jnp.einsum("blhi,dhij->dblhj", xh, w_ig).reshape(2, b, l, LRU_WIDTH)
                       + b_ig[:, None, None])
    log_a = -LRU_C * r * jax.nn.softplus(-lam.astype(jnp.float32))[:, None, None]
    a = jnp.exp(log_a)
    u = jnp.sqrt(-jnp.expm1(2.0 * log_a)) * (i * xc)
    a = jnp.concatenate([a[:1], jnp.flip(a[1:], 2)], 0).astype(jnp.float32)
    u = jnp.concatenate([u[:1], jnp.flip(u[1:], 2)], 0).astype(jnp.float32)
    a_tm = a.transpose(2, 0, 1, 3)
    u_tm = u.transpose(2, 0, 1, 3)
    def step(h, au):
        a_t, u_t = au
        h = a_t * h + u_t
        return h, h
    _, hs = lax.scan(step, jnp.zeros((2, b, LRU_WIDTH), jnp.float32), (a_tm, u_tm))
    hs = hs.transpose(1, 2, 0, 3)
    h = hs[0] + jnp.flip(hs[1], 1)
    return (h * jax.nn.gelu(gate_raw)).astype(x_raw.dtype)


def setup_inputs(seed: int = 0) -> dict:
    key = jax.random.key(seed)
    ks = iter(jax.random.split(key, 40))
    f32 = jnp.float32
    def nrm(shape, s):
        return s * jax.random.normal(next(ks), shape, f32)
    def unif(shape, lo, hi):
        return jax.random.uniform(next(ks), shape, f32, lo, hi)
    dt0 = jnp.exp(unif((DEPTH, 2, SSD_HEADS), math.log(1e-3), math.log(1e-1)))
    lam_s = unif((DEPTH, 2, LRU_WIDTH), 0.9, 0.999) ** (1.0 / LRU_C)
    return {
        "x_prompt": nrm((BATCH, SEQ, D_MODEL), 1.0),
        "x_sample": nrm((DEC_BATCH, DEC_SEQ, D_MODEL), 1.0),
        "meta": nrm((N_META, D_MODEL), 1.0),
        "ln_in_g": 1.0 + nrm((D_MODEL,), 0.02),
        "ln_in_b": nrm((D_MODEL,), 0.02),
        "w_in": nrm((DEPTH, D_MODEL, IN_DIM), D_MODEL ** -0.5),
        "ssd_conv_w": nrm((DEPTH, SSD_CONV, XBC_DIM), SSD_CONV ** -0.5),
        "ssd_conv_b": nrm((DEPTH, XBC_DIM), 0.01),
        "ssd_dt_bias": dt0 + jnp.log(-jnp.expm1(-dt0)),
        "ssd_a_log": jnp.log(unif((DEPTH, 2, SSD_HEADS), 1.0, 16.0)),
        "ssd_d": 1.0 + nrm((DEPTH, SSD_HEADS), 0.02),
        "ssd_norm_g": 1.0 + nrm((DEPTH, SSD_INNER), 0.02),
        "w_ssd_out": nrm((DEPTH, SSD_INNER, D_MODEL), SSD_INNER ** -0.5),
        "lru_conv_w": nrm((DEPTH, LRU_CONV, LRU_WIDTH), LRU_CONV ** -0.5),
        "lru_conv_b": nrm((DEPTH, LRU_WIDTH), 0.01),
        "lru_w_rg": nrm((DEPTH, 2, LRU_HEADS, LRU_BLOCK, LRU_BLOCK), LRU_BLOCK ** -0.5),
        "lru_b_rg": nrm((DEPTH, 2, LRU_WIDTH), 0.01),
        "lru_w_ig": nrm((DEPTH, 2, LRU_HEADS, LRU_BLOCK, LRU_BLOCK), LRU_BLOCK ** -0.5),
        "lru_b_ig": nrm((DEPTH, 2, LRU_WIDTH), 0.01),
        "lru_lambda": jnp.log(lam_s) - jnp.log1p(-lam_s),
        "w_lru_out": nrm((DEPTH, LRU_WIDTH, D_MODEL), LRU_WIDTH ** -0.5),
        "w_o": nrm((DEPTH, D_MODEL, D_MODEL), BETA * D_MODEL ** -0.5),
        "ln1_g": 1.0 + nrm((DEPTH, D_MODEL), 0.02),
        "ln1_b": nrm((DEPTH, D_MODEL), 0.02),
        "w_up": nrm((DEPTH, D_MODEL, 2 * D_FF), D_MODEL ** -0.5),
        "ffn_conv_w": nrm((DEPTH, FFN_CONV, 2 * D_FF), FFN_CONV ** -0.5),
        "ffn_conv_b": nrm((DEPTH, 2 * D_FF), 0.01),
        "w_down": nrm((DEPTH, D_FF, D_MODEL), BETA * D_FF ** -0.5),
        "ln2_g": 1.0 + nrm((DEPTH, D_MODEL), 0.02),
        "ln2_b": nrm((DEPTH, D_MODEL), 0.02),
    }


def reference(x_prompt, x_sample, meta, ln_in_g, ln_in_b, w_in, ssd_conv_w, ssd_conv_b,
              ssd_dt_bias, ssd_a_log, ssd_d, ssd_norm_g, w_ssd_out, lru_conv_w, lru_conv_b,
              lru_w_rg, lru_b_rg, lru_w_ig, lru_b_ig, lru_lambda, w_lru_out, w_o,
              ln1_g, ln1_b, w_up, ffn_conv_w, ffn_conv_b, w_down, ln2_g, ln2_b):
    def run(x):
        b = x.shape[0]
        mt = jnp.broadcast_to(meta.astype(x.dtype)[None], (b, N_META, D_MODEL))
        h = layer_norm(jnp.concatenate([mt, x], 1), ln_in_g, ln_in_b)
        for li in range(DEPTH):
            proj = h @ w_in[li]
            z, xbc, dt_raw, lru_gate, lru_x, merge = jnp.split(proj, [S1, S2, S3, S4, S5], -1)
            y_s = ssd_mixer(z, xbc, dt_raw, ssd_conv_w[li], ssd_conv_b[li], ssd_dt_bias[li],
                            ssd_a_log[li], ssd_d[li], ssd_norm_g[li]) @ w_ssd_out[li]
            y_l = rglru_mixer(lru_gate, lru_x, lru_conv_w[li], lru_conv_b[li], lru_w_rg[li],
                              lru_b_rg[li], lru_w_ig[li], lru_b_ig[li], lru_lambda[li]) @ w_lru_out[li]
            g_s, g_l = jnp.split(jax.nn.sigmoid(merge), 2, -1)
            mix = (g_s * y_s + g_l * y_l) @ w_o[li]
            h = layer_norm(ALPHA * h + mix, ln1_g[li], ln1_b[li])
            up = dwconv(h @ w_up[li], ffn_conv_w[li], ffn_conv_b[li], FFN_CONV // 2)
            u_gate, u_val = jnp.split(up, 2, -1)
            f = (jax.nn.gelu(u_gate) * u_val) @ w_down[li]
            h = layer_norm(ALPHA * h + f, ln2_g[li], ln2_b[li])
        return h[:, N_META:]
    y_prompt = run(x_prompt)
    y_sample = run(x_sample)
    return (y_prompt, y_sample)
```

```python
import functools
import math

import jax
import jax.numpy as jnp
from jax import lax
from jax.experimental import pallas as pl
from jax.experimental.pallas import tpu as pltpu

F32 = jnp.float32
BF16 = jnp.bfloat16

D_MODEL = 1024
DEPTH = 4
N_META = 16
SSD_HEADS = 16
SSD_HEAD_DIM = 64
SSD_INNER = SSD_HEADS * SSD_HEAD_DIM
SSD_GROUPS = 4
SSD_STATE = 128
XBC_DIM = SSD_INNER + 2 * SSD_GROUPS * SSD_STATE
LRU_WIDTH = 1024
LRU_HEADS = 16
LRU_BLOCK = LRU_WIDTH // LRU_HEADS
LRU_C = 8.0
D_FF = 3 * D_MODEL
S1 = SSD_INNER
S2 = S1 + XBC_DIM
S3 = S2 + 2 * SSD_HEADS
S4 = S3 + LRU_WIDTH
S5 = S4 + LRU_WIDTH
ALPHA = (2 * DEPTH) ** 0.25
LN_EPS = 1e-5
RMS_EPS = 1e-6

Q = 128
PAD = Q - N_META
HALO = 8
LANES = 128
ACT_DT = F32
VMEM_LIMIT = 56 * 1024 * 1024


def _cparams(*sem):
    return pltpu.CompilerParams(dimension_semantics=sem, vmem_limit_bytes=VMEM_LIMIT)


def _dot(a, b):
    return jnp.dot(a.astype(BF16), b.astype(BF16), preferred_element_type=F32)


def _dot_nt(a, b):
    return lax.dot_general(a.astype(BF16), b.astype(BF16), (((1,), (1,)), ((), ())),
                           preferred_element_type=F32)


def _split_dot(v, e, terms, e_left=False):
    acc = None
    r = v
    for _ in range(terms):
        hi = r.astype(BF16)
        part = jnp.dot(e, hi, preferred_element_type=F32) if e_left else jnp.dot(hi, e, preferred_element_type=F32)
        acc = part if acc is None else acc + part
        r = r - hi.astype(F32)
    return acc


def _ln(x, g, b):
    mu = jnp.mean(x, -1, keepdims=True)
    xc = x - mu
    var = jnp.mean(xc * xc, -1, keepdims=True)
    return xc * lax.rsqrt(var + LN_EPS) * g + b


def _sigmoid(x):
    return 1.0 / (1.0 + jnp.exp(-x))


def _silu(x):
    return x * _sigmoid(x)


def _gelu(x):
    c = math.sqrt(2.0 / math.pi)
    return x * (0.5 * (1.0 + jnp.tanh(c * (x + 0.044715 * (x * x * x)))))


def _softplus(x):
    return jnp.maximum(x, 0.0) + jnp.log1p(jnp.exp(-jnp.abs(x)))


def _row_iota(n):
    return lax.broadcasted_iota(jnp.int32, (n, 1), 0)


def _shifted(prev8, cur, next8, offsets):
    ext = jnp.concatenate([prev8, cur, next8], axis=0)
    n = ext.shape[0]
    out = []
    for o in offsets:
        if o == 0:
            out.append(cur)
        else:
            out.append(pltpu.roll(ext, (-o) % n, axis=0)[HALO:HALO + Q])
    return out


def _embed_kernel(x_ref, meta_ref, g_ref, b_ref, o_ref):
    c = pl.program_id(1)
    src = jnp.where(c == 0, meta_ref[...], x_ref[...])
    y = _ln(src, g_ref[...], b_ref[...])
    keep = jnp.logical_or(c > 0, _row_iota(Q) >= PAD)
    o_ref[...] = jnp.where(keep, y, 0.0)


def _embed(x, meta_pad, g, b):
    bsz, t, d = x.shape
    nc = t // Q + 1
    return pl.pallas_call(
        _embed_kernel,
        grid=(bsz, nc),
        in_specs=[
            pl.BlockSpec((None, Q, d), lambda i, c: (i, jnp.maximum(c - 1, 0), 0)),
            pl.BlockSpec((Q, d), lambda i, c: (0, 0)),
            pl.BlockSpec((1, d), lambda i, c: (0, 0)),
            pl.BlockSpec((1, d), lambda i, c: (0, 0)),
        ],
        out_specs=pl.BlockSpec((None, Q, d), lambda i, c: (i, c, 0)),
        out_shape=jax.ShapeDtypeStruct((bsz, nc * Q, d), F32),
        compiler_params=_cparams("parallel", "parallel"),
        name="embed_ln",
    )(x, meta_pad, g, b)


def _inproj_kernel(h_ref, wz, wxbc, wdt, wlg, wlx, wmg, z_o, xbc_o, dt_o, lg_o, lx_o, mg_o):
    x = h_ref[...].astype(BF16)
    for w, o in ((wz, z_o), (wxbc, xbc_o), (wdt, dt_o), (wlg, lg_o), (wlx, lx_o), (wmg, mg_o)):
        o[...] = jnp.dot(x, w[...], preferred_element_type=F32).astype(o.dtype)


def _inproj(h2d, ws, tm):
    r, d = h2d.shape
    widths = [w.shape[1] for w in ws]
    dts = [ACT_DT, ACT_DT, F32, ACT_DT, ACT_DT, ACT_DT]
    return pl.pallas_call(
        _inproj_kernel,
        grid=(r // tm,),
        in_specs=[pl.BlockSpec((tm, d), lambda i: (i, 0))]
        + [pl.BlockSpec((d, n), lambda i: (0, 0)) for n in widths],
        out_specs=[pl.BlockSpec((tm, n), lambda i: (i, 0)) for n in widths],
        out_shape=[jax.ShapeDtypeStruct((r, n), dt) for n, dt in zip(widths, dts)],
        compiler_params=_cparams("parallel"),
        name="in_proj",
    )(h2d, *ws)


def _head_onehot(offset):
    r = lax.broadcasted_iota(jnp.int32, (LANES, SSD_INNER), 0)
    c = lax.broadcasted_iota(jnp.int32, (LANES, SSD_INNER), 1)
    return jnp.where(r - offset == c // SSD_HEAD_DIM, 1.0, 0.0).astype(BF16)


def _pair_rhs(x_pair):
    lane = lax.broadcasted_iota(jnp.int32, x_pair.shape, 1)
    lo = jnp.where(lane < SSD_HEAD_DIM, x_pair, 0.0)
    hi = jnp.where(lane >= SSD_HEAD_DIM, x_pair, 0.0)
    return jnp.concatenate([lo, hi], axis=0)


def _ssd_fwd_kernel(xp_ref, xc_ref, xn_ref, dt_ref, cw_ref, cb_ref, dtb_ref, alog_ref,
                    xs_o, cm_o, yp_o, sb_o, aux_o, sf_scr):
    c = pl.program_id(1)
    nc = pl.num_programs(1)

    @pl.when(c == 0)
    def _():
        sf_scr[...] = jnp.zeros_like(sf_scr)

    cur = xc_ref[...].astype(F32)
    nxt = xn_ref[...].astype(F32) * jnp.where(c < nc - 1, 1.0, 0.0)
    taps = _shifted(xp_ref[...].astype(F32), cur, nxt, (-2, -1, 0, 1))
    cw = cw_ref[...]
    acc = cb_ref[...]
    for k in range(4):
        acc = acc + taps[k] * cw[k:k + 1, :]
    act = _silu(acc)
    xs = act[:, :SSD_INNER]
    bm = act[:, SSD_INNER:SSD_INNER + SSD_GROUPS * SSD_STATE]
    cm = act[:, SSD_INNER + SSD_GROUPS * SSD_STATE:]
    xs_o[...] = xs.astype(xs_o.dtype)
    cm_o[...] = cm.astype(cm_o.dtype)

    row = _row_iota(Q)
    lane = lax.broadcasted_iota(jnp.int32, (1, LANES), 1)
    dt = _softplus(dt_ref[...] + dtb_ref[...])
    dt = jnp.where(jnp.logical_or(c > 0, row >= PAD), dt, 0.0)
    a = dt * (-jnp.exp(alog_ref[...]))
    ti = lax.broadcasted_iota(jnp.int32, (Q, Q), 0)
    si = lax.broadcasted_iota(jnp.int32, (Q, Q), 1)
    lower = jnp.where(si <= ti, 1.0, 0.0).astype(BF16)
    upper = jnp.where(si >= ti, 1.0, 0.0).astype(BF16)
    cs = jnp.where(lane < SSD_HEADS, _split_dot(a, lower, 3, True), _split_dot(a, upper, 3, True))
    edge = jnp.where(lane < SSD_HEADS, cs[Q - 1:Q, :], cs[0:1, :])
    wsrc = dt * jnp.exp(edge - cs)
    ecs = jnp.exp(cs)
    aux_o[...] = ecs
    cs_t = cs.T
    dt_t = dt.T
    ecs_b = ecs.astype(BF16)

    lt = si < ti
    eq = si == ti
    sf = sf_scr[...]
    sf_b = sf.astype(BF16)
    xs_b = xs.astype(BF16)
    ys = []
    for j in range(SSD_HEADS // 2):
        g = (2 * j) // (SSD_HEADS // SSD_GROUPS)
        c_g = cm[:, g * SSD_STATE:(g + 1) * SSD_STATE]
        b_g = bm[:, g * SSD_STATE:(g + 1) * SSD_STATE]
        gmat = _dot_nt(c_g, b_g)
        lhs = []
        for h in (2 * j, 2 * j + 1):
            hb = SSD_HEADS + h
            e = jnp.where(lt, cs[:, h:h + 1] - cs_t[h:h + 1, :], cs[:, hb:hb + 1] - cs_t[hb:hb + 1, :])
            dsel = jnp.where(lt, dt_t[h:h + 1, :], dt_t[hb:hb + 1, :]) + jnp.where(eq, dt_t[h:h + 1, :], 0.0)
            lhs.append((gmat * jnp.exp(e) * dsel).astype(BF16))
        for h in (2 * j, 2 * j + 1):
            lhs.append((c_g * ecs[:, h:h + 1]).astype(BF16))
        lhs = jnp.concatenate(lhs, axis=1)
        sl = slice(2 * j * SSD_HEAD_DIM, (2 * j + 2) * SSD_HEAD_DIM)
        rhs = jnp.concatenate([_pair_rhs(xs_b[:, sl]), _pair_rhs(sf_b[:, sl])], axis=0)
        ys.append(jnp.dot(lhs, rhs, preferred_element_type=F32))
    yp_o[...] = jnp.concatenate(ys, axis=1)

    wf = _split_dot(wsrc, _head_onehot(0), 2)
    wb = _split_dot(wsrc, _head_onehot(SSD_HEADS), 2)
    dec = _split_dot(jnp.broadcast_to(ecs[Q - 1:Q, :], (8, LANES)), _head_onehot(0), 2)[0:1, :]
    xf = (xs * wf).astype(BF16)
    xb = (xs * wb).astype(BF16)
    st_f = []
    st_b = []
    for g in range(SSD_GROUPS):
        b_gt = bm[:, g * SSD_STATE:(g + 1) * SSD_STATE].T.astype(BF16)
        sl = slice(g * 256, (g + 1) * 256)
        st_f.append(jnp.dot(b_gt, xf[:, sl], preferred_element_type=F32))
        st_b.append(jnp.dot(b_gt, xb[:, sl], preferred_element_type=F32))
    sb_o[...] = jnp.concatenate(st_b, axis=1)
    sf_scr[...] = sf * dec + jnp.concatenate(st_f, axis=1)


def _ssd_fwd(xbc, dt, conv_w, conv_b, dt_bias, a_log):
    bsz, lp, _ = xbc.shape
    nc = lp // Q
    nh = lp // HALO
    per = Q // HALO
    seq = lambda n: pl.BlockSpec((None, Q, n), lambda i, c: (i, c, 0))
    par = lambda s: pl.BlockSpec(s, lambda i, c: (0, 0))
    return pl.pallas_call(
        _ssd_fwd_kernel,
        grid=(bsz, nc),
        in_specs=[
            pl.BlockSpec((None, HALO, XBC_DIM), lambda i, c: (i, jnp.maximum(c * per - 1, 0), 0)),
            seq(XBC_DIM),
            pl.BlockSpec((None, HALO, XBC_DIM), lambda i, c: (i, jnp.minimum((c + 1) * per, nh - 1), 0)),
            seq(LANES),
            par((4, XBC_DIM)), par((1, XBC_DIM)), par((1, LANES)), par((1, LANES)),
        ],
        out_specs=[
            seq(SSD_INNER), seq(SSD_GROUPS * SSD_STATE), seq(SSD_INNER),
            pl.BlockSpec((None, None, SSD_STATE, SSD_INNER), lambda i, c: (i, c, 0, 0)),
            seq(LANES),
        ],
        out_shape=[
            jax.ShapeDtypeStruct((bsz, lp, SSD_INNER), ACT_DT),
            jax.ShapeDtypeStruct((bsz, lp, SSD_GROUPS * SSD_STATE), ACT_DT),
            jax.ShapeDtypeStruct((bsz, lp, SSD_INNER), F32),
            jax.ShapeDtypeStruct((bsz, nc, SSD_STATE, SSD_INNER), F32),
            jax.ShapeDtypeStruct((bsz, lp, LANES), F32),
        ],
        scratch_shapes=[pltpu.VMEM((SSD_STATE, SSD_INNER), F32)],
        compiler_params=_cparams("parallel", "arbitrary"),
        name="ssd_fwd",
    )(xbc, xbc, xbc, dt, conv_w, conv_b, dt_bias, a_log)


def _ssd_bwd_kernel(xs_ref, cm_ref, yp_ref, sb_ref, aux_ref, z_ref, dskip_ref, ng_ref, wo_ref,
                    ys_o, sb_scr):
    c = pl.program_id(1)

    @pl.when(c == 0)
    def _():
        sb_scr[...] = jnp.zeros_like(sb_scr)

    ecs = aux_ref[...]
    cm = cm_ref[...].astype(F32)
    sb = sb_scr[...]
    sb_b = sb.astype(BF16)
    ys = []
    for j in range(SSD_HEADS // 2):
        g = (2 * j) // (SSD_HEADS // SSD_GROUPS)
        c_g = cm[:, g * SSD_STATE:(g + 1) * SSD_STATE]
        lhs = [(c_g * ecs[:, SSD_HEADS + h:SSD_HEADS + h + 1]).astype(BF16) for h in (2 * j, 2 * j + 1)]
        sl = slice(2 * j * SSD_HEAD_DIM, (2 * j + 2) * SSD_HEAD_DIM)
        ys.append(jnp.dot(jnp.concatenate(lhs, axis=1), _pair_rhs(sb_b[:, sl]), preferred_element_type=F32))
    xs = xs_ref[...].astype(F32)
    y = yp_ref[...] + jnp.concatenate(ys, axis=1) + xs * dskip_ref[...]
    y = y * _silu(z_ref[...].astype(F32))
    y = y * lax.rsqrt(jnp.mean(y * y, -1, keepdims=True) + RMS_EPS) * ng_ref[...]
    ys_o[...] = _dot(y, wo_ref[...]).astype(ys_o.dtype)

    dec = _split_dot(jnp.broadcast_to(ecs[0:1, :], (8, LANES)), _head_onehot(SSD_HEADS), 2)[0:1, :]
    sb_scr[...] = sb * dec + sb_ref[...]


def _ssd_bwd(xs, cm, yp, sb, aux, z, dskip, norm_g, w_out):
    bsz, lp, _ = xs.shape
    nc = lp // Q
    seq = lambda n: pl.BlockSpec((None, Q, n), lambda i, c: (i, nc - 1 - c, 0))
    par = lambda s: pl.BlockSpec(s, lambda i, c: (0, 0))
    return pl.pallas_call(
        _ssd_bwd_kernel,
        grid=(bsz, nc),
        in_specs=[
            seq(SSD_INNER), seq(SSD_GROUPS * SSD_STATE), seq(SSD_INNER),
            pl.BlockSpec((None, None, SSD_STATE, SSD_INNER), lambda i, c: (i, nc - 1 - c, 0, 0)),
            seq(LANES), seq(SSD_INNER),
            par((1, SSD_INNER)), par((1, SSD_INNER)), par((SSD_INNER, D_MODEL)),
        ],
        out_specs=seq(D_MODEL),
        out_shape=jax.ShapeDtypeStruct((bsz, lp, D_MODEL), ACT_DT),
        scratch_shapes=[pltpu.VMEM((SSD_STATE, SSD_INNER), F32)],
        compiler_params=_cparams("parallel", "arbitrary"),
        name="ssd_bwd",
    )(xs, cm, yp, sb, aux, z, dskip, norm_g, w_out)


def _lru_gates(xc, wg_ref, brg, big, lam):
    xb = xc.astype(BF16)
    pre = []
    for j in range(LRU_WIDTH // 256):
        pre.append(jnp.dot(xb[:, j * 256:(j + 1) * 256], wg_ref[j], preferred_element_type=F32))
    r = _sigmoid(jnp.concatenate([p[:, :256] for p in pre], axis=1) + brg)
    i = _sigmoid(jnp.concatenate([p[:, 256:] for p in pre], axis=1) + big)
    log_a = (-LRU_C) * r * _softplus(-lam)
    a = jnp.exp(log_a)
    u = jnp.sqrt(-jnp.tanh(log_a) * (a * a + 1.0)) * (i * xc)
    return a, u


def _lru_scan(a_scr, u_scr, o_ref, h_scr, reverse):
    def body(k, h):
        base = pl.multiple_of((Q // 8 - 1 - k if reverse else k) * 8, 8)
        a8 = a_scr[pl.ds(base, 8), :]
        u8 = u_scr[pl.ds(base, 8), :]
        rows = [None] * 8
        for r in (range(7, -1, -1) if reverse else range(8)):
            h = a8[r:r + 1, :] * h + u8[r:r + 1, :]
            rows[r] = h
        o_ref[pl.ds(base, 8), :] = jnp.concatenate(rows, axis=0)
        return h
    h_scr[...] = lax.fori_loop(0, Q // 8, body, h_scr[...])


def _lru_fwd_kernel(xp_ref, xc_ref, xn_ref, cw_ref, cb_ref, wg_ref, brg_ref, big_ref, lam_ref,
                    xc_o, hf_o, a_scr, u_scr, h_scr):
    c = pl.program_id(1)
    nc = pl.num_programs(1)

    @pl.when(c == 0)
    def _():
        h_scr[...] = jnp.zeros_like(h_scr)

    cur = xc_ref[...].astype(F32)
    nxt = xn_ref[...].astype(F32) * jnp.where(c < nc - 1, 1.0, 0.0)
    taps = _shifted(xp_ref[...].astype(F32), cur, nxt, (-2, -1, 0, 1))
    cw = cw_ref[...]
    xc = cb_ref[...]
    for k in range(4):
        xc = xc + taps[k] * cw[k:k + 1, :]
    xc_o[...] = xc
    a, u = _lru_gates(xc, wg_ref, brg_ref[...], big_ref[...], lam_ref[...])
    a_scr[...] = a
    u_scr[...] = jnp.where(jnp.logical_or(c > 0, _row_iota(Q) >= PAD), u, 0.0)
    _lru_scan(a_scr, u_scr, hf_o, h_scr, reverse=False)


def _lru_fwd(lx, conv_w, conv_b, wg, brg, big, lam):
    bsz, lp, w = lx.shape
    nc = lp // Q
    nh = lp // HALO
    per = Q // HALO
    seq = lambda n: pl.BlockSpec((None, Q, n), lambda i, c: (i, c, 0))
    par = lambda s: pl.BlockSpec(s, lambda i, c: (0,) * len(s))
    return pl.pallas_call(
        _lru_fwd_kernel,
        grid=(bsz, nc),
        in_specs=[
            pl.BlockSpec((None, HALO, w), lambda i, c: (i, jnp.maximum(c * per - 1, 0), 0)),
            seq(w),
            pl.BlockSpec((None, HALO, w), lambda i, c: (i, jnp.minimum((c + 1) * per, nh - 1), 0)),
            par((4, w)), par((1, w)), par((w // 256, 256, 512)), par((1, w)), par((1, w)), par((1, w)),
        ],
        out_specs=[seq(w), seq(w)],
        out_shape=[jax.ShapeDtypeStruct((bsz, lp, w), F32), jax.ShapeDtypeStruct((bsz, lp, w), F32)],
        scratch_shapes=[pltpu.VMEM((Q, w), F32), pltpu.VMEM((Q, w), F32), pltpu.VMEM((1, w), F32)],
        compiler_params=_cparams("parallel", "arbitrary"),
        name="lru_fwd",
    )(lx, lx, lx, conv_w, conv_b, wg, brg, big, lam)


def _lru_bwd_kernel(xc_ref, hf_ref, lg_ref, wg_ref, brg_ref, big_ref, lam_ref, wo_ref,
                    yl_o, a_scr, u_scr, hb_scr, h_scr):
    c = pl.program_id(1)

    @pl.when(c == 0)
    def _():
        h_scr[...] = jnp.zeros_like(h_scr)

    a, u = _lru_gates(xc_ref[...], wg_ref, brg_ref[...], big_ref[...], lam_ref[...])
    a_scr[...] = a
    u_scr[...] = u
    _lru_scan(a_scr, u_scr, hb_scr, h_scr, reverse=True)
    y = (hf_ref[...] + hb_scr[...]) * _gelu(lg_ref[...].astype(F32))
    yl_o[...] = _dot(y, wo_ref[...]).astype(yl_o.dtype)


def _lru_bwd(xc, hf, lg, wg, brg, big, lam, w_out):
    bsz, lp, w = xc.shape
    nc = lp // Q
    seq = lambda n: pl.BlockSpec((None, Q, n), lambda i, c: (i, nc - 1 - c, 0))
    par = lambda s: pl.BlockSpec(s, lambda i, c: (0,) * len(s))
    return pl.pallas_call(
        _lru_bwd_kernel,
        grid=(bsz, nc),
        in_specs=[
            seq(w), seq(w), seq(w),
            par((w // 256, 256, 512)), par((1, w)), par((1, w)), par((1, w)), par((w, D_MODEL)),
        ],
        out_specs=seq(D_MODEL),
        out_shape=jax.ShapeDtypeStruct((bsz, lp, D_MODEL), ACT_DT),
        scratch_shapes=[pltpu.VMEM((Q, w), F32), pltpu.VMEM((Q, w), F32), pltpu.VMEM((Q, w), F32),
                        pltpu.VMEM((1, w), F32)],
        compiler_params=_cparams("parallel", "arbitrary"),
        name="lru_bwd",
    )(xc, hf, lg, wg, brg, big, lam, w_out)


def _merge_kernel(h_ref, ys_ref, yl_ref, mg_ref, wo_ref, g_ref, b_ref, o_ref):
    c = pl.program_id(1)
    mg = mg_ref[...].astype(F32)
    mix = (_sigmoid(mg[:, :D_MODEL]) * ys_ref[...].astype(F32)
           + _sigmoid(mg[:, D_MODEL:]) * yl_ref[...].astype(F32))
    y = _ln(ALPHA * h_ref[...] + _dot(mix, wo_ref[...]), g_ref[...], b_ref[...])
    o_ref[...] = jnp.where(jnp.logical_or(c > 0, _row_iota(Q) >= PAD), y, 0.0)


def _merge(h, ys, yl, mg, w_o, g, b):
    bsz, lp, d = h.shape
    seq = lambda n: pl.BlockSpec((None, Q, n), lambda i, c: (i, c, 0))
    par = lambda s: pl.BlockSpec(s, lambda i, c: (0, 0))
    return pl.pallas_call(
        _merge_kernel,
        grid=(bsz, lp // Q),
        in_specs=[seq(d), seq(d), seq(d), seq(2 * d), par((d, d)), par((1, d)), par((1, d))],
        out_specs=seq(d),
        out_shape=jax.ShapeDtypeStruct((bsz, lp, d), F32),
        compiler_params=_cparams("parallel", "parallel"),
        name="merge_ln",
    )(h, ys, yl, mg, w_o, g, b)


def _ffn_kernel(hp_ref, hc_ref, hn_ref, wu_ref, cw_ref, cb_ref, wd_ref, g_ref, b_ref, o_ref):
    c = pl.program_id(1)
    nc = pl.num_programs(1)
    cur = hc_ref[...]
    nxt = hn_ref[...] * jnp.where(c < nc - 1, 1.0, 0.0)
    ext = jnp.concatenate([hp_ref[...], cur, nxt], axis=0)
    up = _dot(ext, wu_ref[...])
    n = ext.shape[0]
    cw = cw_ref[...]
    acc = cb_ref[...] + up[HALO:HALO + Q] * cw[1:2, :]
    acc = acc + pltpu.roll(up, 1, axis=0)[HALO:HALO + Q] * cw[0:1, :]
    acc = acc + pltpu.roll(up, n - 1, axis=0)[HALO:HALO + Q] * cw[2:3, :]
    f = _dot(_gelu(acc[:, :D_FF]) * acc[:, D_FF:], wd_ref[...])
    y = _ln(ALPHA * cur + f, g_ref[...], b_ref[...])
    o_ref[...] = jnp.where(jnp.logical_or(c > 0, _row_iota(Q) >= PAD), y, 0.0)


def _ffn(h, w_up, conv_w, conv_b, w_down, g, b):
    bsz, lp, d = h.shape
    nh = lp // HALO
    per = Q // HALO
    seq = pl.BlockSpec((None, Q, d), lambda i, c: (i, c, 0))
    par = lambda s: pl.BlockSpec(s, lambda i, c: (0, 0))
    return pl.pallas_call(
        _ffn_kernel,
        grid=(bsz, lp // Q),
        in_specs=[
            pl.BlockSpec((None, HALO, d), lambda i, c: (i, jnp.maximum(c * per - 1, 0), 0)),
            seq,
            pl.BlockSpec((None, HALO, d), lambda i, c: (i, jnp.minimum((c + 1) * per, nh - 1), 0)),
            par((d, 2 * D_FF)), par((3, 2 * D_FF)), par((1, 2 * D_FF)), par((D_FF, d)),
            par((1, d)), par((1, d)),
        ],
        out_specs=seq,
        out_shape=jax.ShapeDtypeStruct((bsz, lp, d), F32),
        compiler_params=_cparams("parallel", "parallel"),
        name="ffn_ln",
    )(h, h, h, w_up, conv_w, conv_b, w_down, g, b)


def _pad_lanes(v):
    return jnp.pad(v.reshape(1, -1).astype(F32), ((0, 0), (0, LANES - v.size)))


def _gate_tiles(w_rg, w_ig):
    def tiles(w):
        w4 = w.reshape(LRU_HEADS // 4, 4, LRU_BLOCK, LRU_BLOCK)
        eye = jnp.eye(4, dtype=w.dtype)
        return jnp.einsum("thij,hk->thikj", w4, eye).reshape(LRU_HEADS // 4, 256, 256)
    return jnp.concatenate([tiles(w_rg), tiles(w_ig)], axis=-1).astype(BF16)


def _prep_layer(p, li):
    w_in = p["w_in"][li]
    row = lambda v: v.reshape(1, -1).astype(F32)
    return dict(
        w_in=[w_in[:, :S1].astype(BF16), w_in[:, S1:S2].astype(BF16),
              jnp.pad(w_in[:, S2:S3], ((0, 0), (0, LANES - (S3 - S2)))).astype(BF16),
              w_in[:, S3:S4].astype(BF16), w_in[:, S4:S5].astype(BF16), w_in[:, S5:].astype(BF16)],
        ssd_conv_w=p["ssd_conv_w"][li], ssd_conv_b=row(p["ssd_conv_b"][li]),
        dt_bias=_pad_lanes(p["ssd_dt_bias"][li]), a_log=_pad_lanes(p["ssd_a_log"][li]),
        dskip=row(jnp.repeat(p["ssd_d"][li], SSD_HEAD_DIM)), norm_g=row(p["ssd_norm_g"][li]),
        w_ssd_out=p["w_ssd_out"][li].astype(BF16),
        lru_conv_w=p["lru_conv_w"][li], lru_conv_b=row(p["lru_conv_b"][li]),
        wg=[_gate_tiles(p["lru_w_rg"][li, d], p["lru_w_ig"][li, d]) for d in range(2)],
        brg=[row(p["lru_b_rg"][li, d]) for d in range(2)],
        big=[row(p["lru_b_ig"][li, d]) for d in range(2)],
        lam=[row(p["lru_lambda"][li, d]) for d in range(2)],
        w_lru_out=p["w_lru_out"][li].astype(BF16), w_o=p["w_o"][li].astype(BF16),
        ln1_g=row(p["ln1_g"][li]), ln1_b=row(p["ln1_b"][li]),
        w_up=p["w_up"][li].astype(BF16), ffn_conv_w=p["ffn_conv_w"][li],
        ffn_conv_b=row(p["ffn_conv_b"][li]), w_down=p["w_down"][li].astype(BF16),
        ln2_g=row(p["ln2_g"][li]), ln2_b=row(p["ln2_b"][li]),
    )


def _row_tile(rows):
    for tm in (512, 256, 128):
        if rows % tm == 0:
            return tm
    raise ValueError(rows)


def _layer(h, lp_):
    bsz, lp, d = h.shape
    z, xbc, dt, lg, lx, mg = _inproj(h.reshape(bsz * lp, d), lp_["w_in"], _row_tile(bsz * lp))
    sh = lambda t: t.reshape(bsz, lp, t.shape[-1])
    xs, cm, yp, sb, aux = _ssd_fwd(sh(xbc), sh(dt), lp_["ssd_conv_w"], lp_["ssd_conv_b"],
                                   lp_["dt_bias"], lp_["a_log"])
    ys = _ssd_bwd(xs, cm, yp, sb, aux, sh(z), lp_["dskip"], lp_["norm_g"], lp_["w_ssd_out"])
    xc, hf = _lru_fwd(sh(lx), lp_["lru_conv_w"], lp_["lru_conv_b"], lp_["wg"][0], lp_["brg"][0],
                      lp_["big"][0], lp_["lam"][0])
    yl = _lru_bwd(xc, hf, sh(lg), lp_["wg"][1], lp_["brg"][1], lp_["big"][1], lp_["lam"][1],
                  lp_["w_lru_out"])
    h = _merge(h, ys, yl, sh(mg), lp_["w_o"], lp_["ln1_g"], lp_["ln1_b"])
    return _ffn(h, lp_["w_up"], lp_["ffn_conv_w"], lp_["ffn_conv_b"], lp_["w_down"],
                lp_["ln2_g"], lp_["ln2_b"])


def _run(x, meta_pad, ln_g, ln_b, layers):
    h = _embed(x, meta_pad, ln_g, ln_b)
    for lp_ in layers:
        h = _layer(h, lp_)
    return h[:, Q:]


def kernel(x_prompt, x_sample, meta, ln_in_g, ln_in_b, w_in, ssd_conv_w, ssd_conv_b, ssd_dt_bias, ssd_a_log, ssd_d, ssd_norm_g, w_ssd_out, lru_conv_w, lru_conv_b, lru_w_rg, lru_b_rg, lru_w_ig, lru_b_ig, lru_lambda, w_lru_out, w_o, ln1_g, ln1_b, w_up, ffn_conv_w, ffn_conv_b, w_down, ln2_g, ln2_b):
    p = dict(w_in=w_in, ssd_conv_w=ssd_conv_w, ssd_conv_b=ssd_conv_b, ssd_dt_bias=ssd_dt_bias,
             ssd_a_log=ssd_a_log, ssd_d=ssd_d, ssd_norm_g=ssd_norm_g, w_ssd_out=w_ssd_out,
             lru_conv_w=lru_conv_w, lru_conv_b=lru_conv_b, lru_w_rg=lru_w_rg, lru_b_rg=lru_b_rg,
             lru_w_ig=lru_w_ig, lru_b_ig=lru_b_ig, lru_lambda=lru_lambda, w_lru_out=w_lru_out,
             w_o=w_o, ln1_g=ln1_g, ln1_b=ln1_b, w_up=w_up, ffn_conv_w=ffn_conv_w,
             ffn_conv_b=ffn_conv_b, w_down=w_down, ln2_g=ln2_g, ln2_b=ln2_b)
    layers = [_prep_layer(p, li) for li in range(w_in.shape[0])]
    meta_pad = jnp.pad(meta.astype(F32), ((PAD, 0), (0, 0)))
    g = ln_in_g.reshape(1, -1)
    b = ln_in_b.reshape(1, -1)
    return (_run(x_prompt, meta_pad, g, b, layers), _run(x_sample, meta_pad, g, b, layers))
```

```python
import math

import jax
import jax.numpy as jnp
from jax import lax
from jax.experimental import pallas as pl
from jax.experimental.pallas import tpu as pltpu

F32 = jnp.float32
BF16 = jnp.bfloat16

D_MODEL = 1024
DEPTH = 4
N_META = 16
SSD_HEADS = 16
SSD_HEAD_DIM = 64
SSD_INNER = SSD_HEADS * SSD_HEAD_DIM
SSD_GROUPS = 4
SSD_STATE = 128
BC_DIM = SSD_GROUPS * SSD_STATE
XBC_DIM = SSD_INNER + 2 * BC_DIM
LRU_WIDTH = 1024
LRU_HEADS = 16
LRU_BLOCK = LRU_WIDTH // LRU_HEADS
LRU_C = 8.0
D_FF = 3 * D_MODEL
S1 = SSD_INNER
S2 = S1 + XBC_DIM
S3 = S2 + 2 * SSD_HEADS
S4 = S3 + LRU_WIDTH
S5 = S4 + LRU_WIDTH
ALPHA = (2 * DEPTH) ** 0.25
LN_EPS = 1e-5
RMS_EPS = 1e-6

Q = 128
PAD = Q - N_META
HALO = 8
LANES = 128
FFN_COLS = 512
ACT_DT = BF16
VMEM_LIMIT = 56 * 1024 * 1024


def _cparams(*sem):
    return pltpu.CompilerParams(dimension_semantics=sem, vmem_limit_bytes=VMEM_LIMIT)


def _const_spec(shape):
    zeros = (0,) * len(shape)
    return pl.BlockSpec(shape, lambda i, c: zeros, pipeline_mode=pl.Buffered(1))


def _dot(a, b):
    return jnp.dot(a.astype(BF16), b.astype(BF16), preferred_element_type=F32)


def _dot_nt(a, b):
    return lax.dot_general(a.astype(BF16), b.astype(BF16), (((1,), (1,)), ((), ())),
                           preferred_element_type=F32)


def _split_dot(v, e, terms, e_left=False):
    acc = None
    r = v
    for _ in range(terms):
        hi = r.astype(BF16)
        part = (jnp.dot(e, hi, preferred_element_type=F32) if e_left
                else jnp.dot(hi, e, preferred_element_type=F32))
        acc = part if acc is None else acc + part
        r = r - hi.astype(F32)
    return acc


def _ln(x, g, b):
    mu = jnp.mean(x, -1, keepdims=True)
    xc = x - mu
    var = jnp.mean(xc * xc, -1, keepdims=True)
    return xc * lax.rsqrt(var + LN_EPS) * g + b


def _sigmoid(x):
    return 1.0 / (1.0 + jnp.exp(-x))


def _silu(x):
    return x * _sigmoid(x)


def _gelu(x):
    c = math.sqrt(2.0 / math.pi)
    return x * (0.5 * (1.0 + jnp.tanh(c * (x + 0.044715 * (x * x * x)))))


def _log1p(e):
    u = 1.0 + e
    return jnp.where(u == 1.0, e, jnp.log(u) * (e / (u - 1.0)))


def _softplus(x):
    return jnp.maximum(x, 0.0) + _log1p(jnp.exp(-jnp.abs(x)))


def _row_iota(n):
    return lax.broadcasted_iota(jnp.int32, (n, 1), 0)


def _tile_rows(ext, o):
    if o == 0:
        return ext[HALO:HALO + Q]
    return pltpu.roll(ext, (-o) % ext.shape[0], axis=0)[HALO:HALO + Q]


def _dwconv4(ext, cw, cb):
    acc = cb + _tile_rows(ext, 0) * cw[2:3, :]
    for k, o in ((0, -2), (1, -1), (3, 1)):
        acc = acc + _tile_rows(ext, o) * cw[k:k + 1, :]
    return acc


def _embed_kernel(x_ref, meta_ref, g_ref, b_ref, o_ref):
    c = pl.program_id(1)
    src = jnp.where(c == 0, meta_ref[...], x_ref[...])
    y = _ln(src, g_ref[...], b_ref[...])
    keep = jnp.logical_or(c > 0, _row_iota(Q) >= PAD)
    o_ref[...] = jnp.where(keep, y, 0.0)


def _embed(x, meta_pad, g, b):
    bsz, t, d = x.shape
    nc = t // Q + 1
    return pl.pallas_call(
        _embed_kernel,
        grid=(bsz, nc),
        in_specs=[
            pl.BlockSpec((None, Q, d), lambda i, c: (i, jnp.maximum(c - 1, 0), 0)),
            pl.BlockSpec((Q, d), lambda i, c: (0, 0)),
            pl.BlockSpec((1, d), lambda i, c: (0, 0)),
            pl.BlockSpec((1, d), lambda i, c: (0, 0)),
        ],
        out_specs=pl.BlockSpec((None, Q, d), lambda i, c: (i, c, 0)),
        out_shape=jax.ShapeDtypeStruct((bsz, nc * Q, d), F32),
        compiler_params=_cparams("parallel", "parallel"),
        name="embed_ln",
    )(x, meta_pad, g, b)


def _head_onehot(offset):
    r = lax.broadcasted_iota(jnp.int32, (LANES, SSD_INNER), 0)
    c = lax.broadcasted_iota(jnp.int32, (LANES, SSD_INNER), 1)
    return jnp.where(r - offset == c // SSD_HEAD_DIM, 1.0, 0.0).astype(BF16)


def _pair_rhs(x_pair):
    lane = lax.broadcasted_iota(jnp.int32, x_pair.shape, 1)
    lo = jnp.where(lane < SSD_HEAD_DIM, x_pair, 0.0)
    hi = jnp.where(lane >= SSD_HEAD_DIM, x_pair, 0.0)
    return jnp.concatenate([lo, hi], axis=0)


def _ssd_forward(act, dt_raw, dtb, alog, first, sf_scr):
    xs = act[:, :SSD_INNER]
    bm = act[:, SSD_INNER:SSD_INNER + BC_DIM]
    cm = act[:, SSD_INNER + BC_DIM:]
    row = _row_iota(Q)
    lane = lax.broadcasted_iota(jnp.int32, (1, LANES), 1)
    dt = _softplus(dt_raw + dtb)
    dt = jnp.where(jnp.logical_or(jnp.logical_not(first), row >= PAD), dt, 0.0)
    a = dt * (-jnp.exp(alog))
    ti = lax.broadcasted_iota(jnp.int32, (Q, Q), 0)
    si = lax.broadcasted_iota(jnp.int32, (Q, Q), 1)
    lower = jnp.where(si <= ti, 1.0, 0.0).astype(BF16)
    upper = jnp.where(si >= ti, 1.0, 0.0).astype(BF16)
    cs = jnp.where(lane < SSD_HEADS, _split_dot(a, lower, 3, True), _split_dot(a, upper, 3, True))
    edge = jnp.where(lane < SSD_HEADS, cs[Q - 1:Q, :], cs[0:1, :])
    wsrc = dt * jnp.exp(edge - cs)
    ecs = jnp.exp(cs)
    cs_t = cs.T
    dt_t = dt.T

    lt = si < ti
    eq = si == ti
    sf = sf_scr[...]
    sf_b = sf.astype(BF16)
    xs_b = xs.astype(BF16)
    ys = []
    for j in range(SSD_HEADS // 2):
        g = (2 * j) // (SSD_HEADS // SSD_GROUPS)
        c_g = cm[:, g * SSD_STATE:(g + 1) * SSD_STATE]
        b_g = bm[:, g * SSD_STATE:(g + 1) * SSD_STATE]
        gmat = _dot_nt(c_g, b_g)
        lhs = []
        for h in (2 * j, 2 * j + 1):
            hb = SSD_HEADS + h
            e = jnp.where(lt, cs[:, h:h + 1] - cs_t[h:h + 1, :], cs[:, hb:hb + 1] - cs_t[hb:hb + 1, :])
            dsel = (jnp.where(lt, dt_t[h:h + 1, :], dt_t[hb:hb + 1, :])
                    + jnp.where(eq, dt_t[h:h + 1, :], 0.0))
            lhs.append((gmat * jnp.exp(e) * dsel).astype(BF16))
        for h in (2 * j, 2 * j + 1):
            lhs.append((c_g * ecs[:, h:h + 1]).astype(BF16))
        lhs = jnp.concatenate(lhs, axis=1)
        sl = slice(2 * j * SSD_HEAD_DIM, (2 * j + 2) * SSD_HEAD_DIM)
        rhs = jnp.concatenate([_pair_rhs(xs_b[:, sl]), _pair_rhs(sf_b[:, sl])], axis=0)
        ys.append(jnp.dot(lhs, rhs, preferred_element_type=F32))
    y_part = jnp.concatenate(ys, axis=1)

    wf = _split_dot(wsrc, _head_onehot(0), 2)
    wb = _split_dot(wsrc, _head_onehot(SSD_HEADS), 2)
    dec = _split_dot(jnp.broadcast_to(ecs[Q - 1:Q, :], (8, LANES)), _head_onehot(0), 2)[0:1, :]
    xf = (xs * wf).astype(BF16)
    xb = (xs * wb).astype(BF16)
    st_f = []
    st_b = []
    for g in range(SSD_GROUPS):
        b_gt = bm[:, g * SSD_STATE:(g + 1) * SSD_STATE].T.astype(BF16)
        sl = slice(g * 256, (g + 1) * 256)
        st_f.append(jnp.dot(b_gt, xf[:, sl], preferred_element_type=F32))
        st_b.append(jnp.dot(b_gt, xb[:, sl], preferred_element_type=F32))
    sf_scr[...] = sf * dec + jnp.concatenate(st_f, axis=1)
    return y_part, jnp.concatenate(st_b, axis=1), ecs


def _ssd_reverse(cm, ecs, sb_chunk, sb_scr):
    sb = sb_scr[...]
    sb_b = sb.astype(BF16)
    ys = []
    for j in range(SSD_HEADS // 2):
        g = (2 * j) // (SSD_HEADS // SSD_GROUPS)
        c_g = cm[:, g * SSD_STATE:(g + 1) * SSD_STATE]
        lhs = [(c_g * ecs[:, SSD_HEADS + h:SSD_HEADS + h + 1]).astype(BF16) for h in (2 * j, 2 * j + 1)]
        sl = slice(2 * j * SSD_HEAD_DIM, (2 * j + 2) * SSD_HEAD_DIM)
        ys.append(jnp.dot(jnp.concatenate(lhs, axis=1), _pair_rhs(sb_b[:, sl]), preferred_element_type=F32))
    dec = _split_dot(jnp.broadcast_to(ecs[0:1, :], (8, LANES)), _head_onehot(SSD_HEADS), 2)[0:1, :]
    sb_scr[...] = sb * dec + sb_chunk
    return jnp.concatenate(ys, axis=1)


def _lru_gates(xc, wg_ref, brg, big, lam):
    xb = xc.astype(BF16)
    pre = []
    for j in range(LRU_WIDTH // 256):
        pre.append(jnp.dot(xb[:, j * 256:(j + 1) * 256], wg_ref[j], preferred_element_type=F32))
    r = _sigmoid(jnp.concatenate([p[:, :256] for p in pre], axis=1) + brg)
    i = _sigmoid(jnp.concatenate([p[:, 256:] for p in pre], axis=1) + big)
    log_a = (-LRU_C) * r * _softplus(-lam)
    a = jnp.exp(log_a)
    u = jnp.sqrt(-jnp.tanh(log_a) * (a * a + 1.0)) * (i * xc)
    return a, u


def _lru_scan(a_scr, u_scr, o_ref, h_scr, reverse):
    def body(k, h):
        base = pl.multiple_of((Q // 8 - 1 - k if reverse else k) * 8, 8)
        for r in (range(7, -1, -1) if reverse else range(8)):
            h = a_scr[pl.ds(base + r, 1), :] * h + u_scr[pl.ds(base + r, 1), :]
            o_ref[pl.ds(base + r, 1), :] = h.astype(o_ref.dtype)
        return h
    h_scr[...] = lax.fori_loop(0, Q // 8, body, h_scr[...])


def _fwd_kernel(hp_ref, hc_ref, hn_ref, wf_ref, scw_ref, scb_ref, dtb_ref, alog_ref,
                lcw_ref, lcb_ref, wg_ref, brg_ref, big_ref, lam_ref,
                xs_o, cm_o, yp_o, sb_o, aux_o, xc_o, hf_o,
                sf_scr, a_scr, u_scr, hrow_scr, h_scr):
    c = pl.program_id(1)
    nc = pl.num_programs(1)
    first = c == 0

    @pl.when(first)
    def _():
        sf_scr[...] = jnp.zeros_like(sf_scr)
        h_scr[...] = jnp.zeros_like(h_scr)

    nxt = hn_ref[...] * jnp.where(c < nc - 1, 1.0, 0.0)
    ext = jnp.concatenate([hp_ref[...], hc_ref[...], nxt], axis=0).astype(BF16)
    pe = jnp.dot(ext, wf_ref[...], preferred_element_type=F32)

    act = _silu(_dwconv4(pe[:, :XBC_DIM], scw_ref[...], scb_ref[...]))
    dt_raw = pe[HALO:HALO + Q, XBC_DIM + LRU_WIDTH:]
    y_part, st_b, ecs = _ssd_forward(act, dt_raw, dtb_ref[...], alog_ref[...], first, sf_scr)
    xs_o[...] = act[:, :SSD_INNER].astype(xs_o.dtype)
    cm_o[...] = act[:, SSD_INNER + BC_DIM:].astype(cm_o.dtype)
    yp_o[...] = y_part.astype(yp_o.dtype)
    sb_o[...] = st_b.astype(sb_o.dtype)
    aux_o[...] = ecs

    xc = _dwconv4(pe[:, XBC_DIM:XBC_DIM + LRU_WIDTH], lcw_ref[...], lcb_ref[...])
    xc_o[...] = xc.astype(xc_o.dtype)
    a, u = _lru_gates(xc, wg_ref, brg_ref[...], big_ref[...], lam_ref[...])
    a_scr[...] = a
    u_scr[...] = jnp.where(jnp.logical_or(jnp.logical_not(first), _row_iota(Q) >= PAD), u, 0.0)
    _lru_scan(a_scr, u_scr, hrow_scr, h_scr, reverse=False)
    hf_o[...] = hrow_scr[...].astype(hf_o.dtype)


def _fwd(h, lw):
    bsz, lp, d = h.shape
    nc = lp // Q
    nh = lp // HALO
    per = Q // HALO
    w = LRU_WIDTH
    seq = lambda n: pl.BlockSpec((None, Q, n), lambda i, c: (i, c, 0))
    return pl.pallas_call(
        _fwd_kernel,
        grid=(bsz, nc),
        in_specs=[
            pl.BlockSpec((None, HALO, d), lambda i, c: (i, jnp.maximum(c * per - 1, 0), 0)),
            seq(d),
            pl.BlockSpec((None, HALO, d), lambda i, c: (i, jnp.minimum((c + 1) * per, nh - 1), 0)),
            _const_spec((d, XBC_DIM + w + LANES)),
            _const_spec((4, XBC_DIM)), _const_spec((1, XBC_DIM)), _const_spec((1, LANES)), _const_spec((1, LANES)),
            _const_spec((4, w)), _const_spec((1, w)), _const_spec((w // 256, 256, 512)),
            _const_spec((1, w)), _const_spec((1, w)), _const_spec((1, w)),
        ],
        out_specs=[
            seq(SSD_INNER), seq(BC_DIM), seq(SSD_INNER),
            pl.BlockSpec((None, None, SSD_STATE, SSD_INNER), lambda i, c: (i, c, 0, 0)),
            seq(LANES), seq(w), seq(w),
        ],
        out_shape=[
            jax.ShapeDtypeStruct((bsz, lp, SSD_INNER), ACT_DT),
            jax.ShapeDtypeStruct((bsz, lp, BC_DIM), ACT_DT),
            jax.ShapeDtypeStruct((bsz, lp, SSD_INNER), ACT_DT),
            jax.ShapeDtypeStruct((bsz, nc, SSD_STATE, SSD_INNER), ACT_DT),
            jax.ShapeDtypeStruct((bsz, lp, LANES), F32),
            jax.ShapeDtypeStruct((bsz, lp, w), ACT_DT),
            jax.ShapeDtypeStruct((bsz, lp, w), ACT_DT),
        ],
        scratch_shapes=[pltpu.VMEM((SSD_STATE, SSD_INNER), F32), pltpu.VMEM((Q, w), F32),
                        pltpu.VMEM((Q, w), F32), pltpu.VMEM((Q, w), F32), pltpu.VMEM((1, w), F32)],
        compiler_params=_cparams("parallel", "arbitrary"),
        name="fwd_sweep",
    )(h, h, h, lw["w_fwd"], lw["ssd_conv_w"], lw["ssd_conv_b"], lw["dt_bias"], lw["a_log"],
      lw["lru_conv_w"], lw["lru_conv_b"], lw["wg"][0], lw["brg"][0], lw["big"][0], lw["lam"][0])


def _rev_kernel(h_ref, xs_ref, cm_ref, yp_ref, sb_ref, aux_ref, xc_ref, hf_ref,
                wr_ref, dskip_ref, ng_ref, wso_ref, wg_ref, brg_ref, big_ref, lam_ref, wlo_ref,
                wo_ref, g_ref, b_ref, o_ref,
                sb_scr, a_scr, u_scr, hrow_scr, h_scr):
    k = pl.program_id(1)
    nc = pl.num_programs(1)

    @pl.when(k == 0)
    def _():
        sb_scr[...] = jnp.zeros_like(sb_scr)
        h_scr[...] = jnp.zeros_like(h_scr)

    h = h_ref[...]
    pr = _dot(h, wr_ref[...])
    z = pr[:, :SSD_INNER]
    lg = pr[:, SSD_INNER:SSD_INNER + LRU_WIDTH]
    mg = pr[:, SSD_INNER + LRU_WIDTH:]

    y_off = _ssd_reverse(cm_ref[...].astype(F32), aux_ref[...], sb_ref[...].astype(F32), sb_scr)
    y = yp_ref[...].astype(F32) + y_off + xs_ref[...].astype(F32) * dskip_ref[...]
    y = y * _silu(z)
    y = y * lax.rsqrt(jnp.mean(y * y, -1, keepdims=True) + RMS_EPS) * ng_ref[...]
    y_s = _dot(y, wso_ref[...])

    a, u = _lru_gates(xc_ref[...].astype(F32), wg_ref, brg_ref[...], big_ref[...], lam_ref[...])
    a_scr[...] = a
    u_scr[...] = u
    _lru_scan(a_scr, u_scr, hrow_scr, h_scr, reverse=True)
    y_l = _dot((hf_ref[...].astype(F32) + hrow_scr[...]) * _gelu(lg), wlo_ref[...])

    mix = _sigmoid(mg[:, :D_MODEL]) * y_s + _sigmoid(mg[:, D_MODEL:]) * y_l
    out = _ln(ALPHA * h + _dot(mix, wo_ref[...]), g_ref[...], b_ref[...])
    o_ref[...] = jnp.where(jnp.logical_or(k < nc - 1, _row_iota(Q) >= PAD), out, 0.0)


def _rev(h, xs, cm, yp, sb, aux, xc, hf, lw):
    bsz, lp, d = h.shape
    nc = lp // Q
    w = LRU_WIDTH
    seq = lambda n: pl.BlockSpec((None, Q, n), lambda i, c: (i, nc - 1 - c, 0))
    return pl.pallas_call(
        _rev_kernel,
        grid=(bsz, nc),
        in_specs=[
            seq(d), seq(SSD_INNER), seq(BC_DIM), seq(SSD_INNER),
            pl.BlockSpec((None, None, SSD_STATE, SSD_INNER), lambda i, c: (i, nc - 1 - c, 0, 0)),
            seq(LANES), seq(w), seq(w),
            _const_spec((d, SSD_INNER + w + 2 * d)),
            _const_spec((1, SSD_INNER)), _const_spec((1, SSD_INNER)), _const_spec((SSD_INNER, d)),
            _const_spec((w // 256, 256, 512)), _const_spec((1, w)), _const_spec((1, w)), _const_spec((1, w)),
            _const_spec((w, d)), _const_spec((d, d)), _const_spec((1, d)), _const_spec((1, d)),
        ],
        out_specs=seq(d),
        out_shape=jax.ShapeDtypeStruct((bsz, lp, d), F32),
        scratch_shapes=[pltpu.VMEM((SSD_STATE, SSD_INNER), F32), pltpu.VMEM((Q, w), F32),
                        pltpu.VMEM((Q, w), F32), pltpu.VMEM((Q, w), F32), pltpu.VMEM((1, w), F32)],
        compiler_params=_cparams("parallel", "arbitrary"),
        name="rev_sweep",
    )(h, xs, cm, yp, sb, aux, xc, hf, lw["w_rev"], lw["dskip"], lw["norm_g"], lw["w_ssd_out"],
      lw["wg"][1], lw["brg"][1], lw["big"][1], lw["lam"][1], lw["w_lru_out"], lw["w_o"],
      lw["ln1_g"], lw["ln1_b"])


def _ffn_kernel(hp_ref, hc_ref, hn_ref, wu_ref, cw_ref, cb_ref, wd_ref, g_ref, b_ref, o_ref):
    c = pl.program_id(1)
    nc = pl.num_programs(1)
    tf = hc_ref.shape[0]
    cur = hc_ref[...]
    nxt = hn_ref[...] * jnp.where(c < nc - 1, 1.0, 0.0)
    ext = jnp.concatenate([hp_ref[...], cur, nxt], axis=0).astype(BF16)
    n = ext.shape[0]

    def conv3(up, col):
        cw = cw_ref[:, pl.ds(col, FFN_COLS)]
        acc = cb_ref[:, pl.ds(col, FFN_COLS)] + up[HALO:HALO + tf] * cw[1:2, :]
        acc = acc + pltpu.roll(up, 1, axis=0)[HALO:HALO + tf] * cw[0:1, :]
        return acc + pltpu.roll(up, n - 1, axis=0)[HALO:HALO + tf] * cw[2:3, :]

    f = None
    for j in range(D_FF // FFN_COLS):
        cg, cv = j * FFN_COLS, D_FF + j * FFN_COLS
        ug = conv3(jnp.dot(ext, wu_ref[:, pl.ds(cg, FFN_COLS)], preferred_element_type=F32), cg)
        uv = conv3(jnp.dot(ext, wu_ref[:, pl.ds(cv, FFN_COLS)], preferred_element_type=F32), cv)
        part = _dot(_gelu(ug) * uv, wd_ref[pl.ds(cg, FFN_COLS), :])
        f = part if f is None else f + part
    y = _ln(ALPHA * cur + f, g_ref[...], b_ref[...])
    o_ref[...] = jnp.where(c * tf + _row_iota(tf) >= PAD, y, 0.0)


def _ffn_tile(lp):
    for parts in (4, 8, 2, 1):
        if lp % (parts * HALO) == 0 and lp // parts <= 640:
            return lp // parts
    return Q


def _ffn(h, lw):
    bsz, lp, d = h.shape
    tf = _ffn_tile(lp)
    nh = lp // HALO
    per = tf // HALO
    seq = pl.BlockSpec((None, tf, d), lambda i, c: (i, c, 0))
    return pl.pallas_call(
        _ffn_kernel,
        grid=(bsz, lp // tf),
        in_specs=[
            pl.BlockSpec((None, HALO, d), lambda i, c: (i, jnp.maximum(c * per - 1, 0), 0)),
            seq,
            pl.BlockSpec((None, HALO, d), lambda i, c: (i, jnp.minimum((c + 1) * per, nh - 1), 0)),
            _const_spec((d, 2 * D_FF)), _const_spec((3, 2 * D_FF)), _const_spec((1, 2 * D_FF)),
            _const_spec((D_FF, d)), _const_spec((1, d)), _const_spec((1, d)),
        ],
        out_specs=seq,
        out_shape=jax.ShapeDtypeStruct((bsz, lp, d), F32),
        compiler_params=_cparams("parallel", "parallel"),
        name="ffn_ln",
    )(h, h, h, lw["w_up"], lw["ffn_conv_w"], lw["ffn_conv_b"], lw["w_down"], lw["ln2_g"], lw["ln2_b"])


def _pad_lanes(v):
    return jnp.pad(v.reshape(1, -1).astype(F32), ((0, 0), (0, LANES - v.size)))


def _gate_tiles(w_rg, w_ig):
    def tiles(w):
        w4 = w.reshape(LRU_HEADS // 4, 4, LRU_BLOCK, LRU_BLOCK)
        eye = jnp.eye(4, dtype=w.dtype)
        return jnp.einsum("thij,hk->thikj", w4, eye).reshape(LRU_HEADS // 4, 256, 256)
    return jnp.concatenate([tiles(w_rg), tiles(w_ig)], axis=-1).astype(BF16)


def _prep_layer(p, li):
    w_in = p["w_in"][li]
    row = lambda v: v.reshape(1, -1).astype(F32)
    w_dt = jnp.pad(w_in[:, S2:S3], ((0, 0), (0, LANES - (S3 - S2))))
    return dict(
        w_fwd=jnp.concatenate([w_in[:, S1:S2], w_in[:, S4:S5], w_dt], axis=1).astype(BF16),
        w_rev=jnp.concatenate([w_in[:, :S1], w_in[:, S3:S4], w_in[:, S5:]], axis=1).astype(BF16),
        ssd_conv_w=p["ssd_conv_w"][li], ssd_conv_b=row(p["ssd_conv_b"][li]),
        dt_bias=_pad_lanes(p["ssd_dt_bias"][li]), a_log=_pad_lanes(p["ssd_a_log"][li]),
        dskip=row(jnp.repeat(p["ssd_d"][li], SSD_HEAD_DIM)), norm_g=row(p["ssd_norm_g"][li]),
        w_ssd_out=p["w_ssd_out"][li].astype(BF16),
        lru_conv_w=p["lru_conv_w"][li], lru_conv_b=row(p["lru_conv_b"][li]),
        wg=[_gate_tiles(p["lru_w_rg"][li, d], p["lru_w_ig"][li, d]) for d in range(2)],
        brg=[row(p["lru_b_rg"][li, d]) for d in range(2)],
        big=[row(p["lru_b_ig"][li, d]) for d in range(2)],
        lam=[row(p["lru_lambda"][li, d]) for d in range(2)],
        w_lru_out=p["w_lru_out"][li].astype(BF16), w_o=p["w_o"][li].astype(BF16),
        ln1_g=row(p["ln1_g"][li]), ln1_b=row(p["ln1_b"][li]),
        w_up=p["w_up"][li].astype(BF16), ffn_conv_w=p["ffn_conv_w"][li],
        ffn_conv_b=row(p["ffn_conv_b"][li]), w_down=p["w_down"][li].astype(BF16),
        ln2_g=row(p["ln2_g"][li]), ln2_b=row(p["ln2_b"][li]),
    )


def _layer(h, lw):
    xs, cm, yp, sb, aux, xc, hf = _fwd(h, lw)
    h = _rev(h, xs, cm, yp, sb, aux, xc, hf, lw)
    return _ffn(h, lw)


def _run(x, meta_pad, ln_g, ln_b, layers):
    h = _embed(x, meta_pad, ln_g, ln_b)
    for lw in layers:
        h = _layer(h, lw)
    return h[:, Q:]


def kernel(x_prompt, x_sample, meta, ln_in_g, ln_in_b, w_in, ssd_conv_w, ssd_conv_b, ssd_dt_bias, ssd_a_log, ssd_d, ssd_norm_g, w_ssd_out, lru_conv_w, lru_conv_b, lru_w_rg, lru_b_rg, lru_w_ig, lru_b_ig, lru_lambda, w_lru_out, w_o, ln1_g, ln1_b, w_up, ffn_conv_w, ffn_conv_b, w_down, ln2_g, ln2_b):
    p = dict(w_in=w_in, ssd_conv_w=ssd_conv_w, ssd_conv_b=ssd_conv_b, ssd_dt_bias=ssd_dt_bias,
             ssd_a_log=ssd_a_log, ssd_d=ssd_d, ssd_norm_g=ssd_norm_g, w_ssd_out=w_ssd_out,
             lru_conv_w=lru_conv_w, lru_conv_b=lru_conv_b, lru_w_rg=lru_w_rg, lru_b_rg=lru_b_rg,
             lru_w_ig=lru_w_ig, lru_b_ig=lru_b_ig, lru_lambda=lru_lambda, w_lru_out=w_lru_out,
             w_o=w_o, ln1_g=ln1_g, ln1_b=ln1_b, w_up=w_up, ffn_conv_w=ffn_conv_w,
             ffn_conv_b=ffn_conv_b, w_down=w_down, ln2_g=ln2_g, ln2_b=ln2_b)
    layers = [_prep_layer(p, li) for li in range(w_in.shape[0])]
    meta_pad = jnp.pad(meta.astype(F32), ((PAD, 0), (0, 0)))
    g = ln_in_g.reshape(1, -1)
    b = ln_in_b.reshape(1, -1)
    return (_run(x_prompt, meta_pad, g, b, layers), _run(x_sample, meta_pad, g, b, layers))
```

```python
import math

import jax
import jax.numpy as jnp
from jax import lax
from jax.experimental import pallas as pl
from jax.experimental.pallas import tpu as pltpu

F32 = jnp.float32
BF16 = jnp.bfloat16

D_MODEL = 1024
DEPTH = 4
N_META = 16
SSD_HEADS = 16
SSD_HEAD_DIM = 64
SSD_INNER = SSD_HEADS * SSD_HEAD_DIM
SSD_GROUPS = 4
SSD_STATE = 128
BC_DIM = SSD_GROUPS * SSD_STATE
XBC_DIM = SSD_INNER + 2 * BC_DIM
LRU_WIDTH = 1024
LRU_HEADS = 16
LRU_BLOCK = LRU_WIDTH // LRU_HEADS
LRU_C = 8.0
D_FF = 3 * D_MODEL
S1 = SSD_INNER
S2 = S1 + XBC_DIM
S3 = S2 + 2 * SSD_HEADS
S4 = S3 + LRU_WIDTH
S5 = S4 + LRU_WIDTH
ALPHA = (2 * DEPTH) ** 0.25
LN_EPS = 1e-5
RMS_EPS = 1e-6

Q = 128
PAD = Q - N_META
HALO = 8
LANES = 128
SUBLANES = 8
SEG_SHIFT = 4
SEG = 1 << SEG_SHIFT
FFN_COLS = 512
ACT_DT = BF16
VMEM_LIMIT = 56 * 1024 * 1024


def _cparams(*sem):
    return pltpu.CompilerParams(dimension_semantics=sem, vmem_limit_bytes=VMEM_LIMIT)


def _const_spec(shape):
    zeros = (0,) * len(shape)
    return pl.BlockSpec(shape, lambda i, c: zeros, pipeline_mode=pl.Buffered(1))


def _dot(a, b):
    return jnp.dot(a.astype(BF16), b.astype(BF16), preferred_element_type=F32)


def _dot_nt(a, b):
    return lax.dot_general(a.astype(BF16), b.astype(BF16), (((1,), (1,)), ((), ())),
                           preferred_element_type=F32)


def _split_dot(v, e, terms, e_left=False):
    acc = None
    r = v
    for _ in range(terms):
        hi = r.astype(BF16)
        part = (jnp.dot(e, hi, preferred_element_type=F32) if e_left
                else jnp.dot(hi, e, preferred_element_type=F32))
        acc = part if acc is None else acc + part
        r = r - hi.astype(F32)
    return acc


def _ln(x, g, b):
    mu = jnp.mean(x, -1, keepdims=True)
    xc = x - mu
    var = jnp.mean(xc * xc, -1, keepdims=True)
    return xc * lax.rsqrt(var + LN_EPS) * g + b


def _sigmoid(x):
    return 1.0 / (1.0 + jnp.exp(-x))


def _silu(x):
    return x * _sigmoid(x)


def _gelu(x):
    c = math.sqrt(2.0 / math.pi)
    return x * (0.5 * (1.0 + jnp.tanh(c * (x + 0.044715 * (x * x * x)))))


def _log1p(e):
    u = 1.0 + e
    return jnp.where(u == 1.0, e, jnp.log(u) * (e / (u - 1.0)))


def _softplus(x):
    return jnp.maximum(x, 0.0) + _log1p(jnp.exp(-jnp.abs(x)))


def _row_iota(n):
    return lax.broadcasted_iota(jnp.int32, (n, 1), 0)


def _tile_rows(ext, o):
    if o == 0:
        return ext[HALO:HALO + Q]
    return pltpu.roll(ext, (-o) % ext.shape[0], axis=0)[HALO:HALO + Q]


def _dwconv4(ext, cw, cb):
    acc = cb + _tile_rows(ext, 0) * cw[2:3, :]
    for k, o in ((0, -2), (1, -1), (3, 1)):
        acc = acc + _tile_rows(ext, o) * cw[k:k + 1, :]
    return acc


def _embed_kernel(x_ref, meta_ref, g_ref, b_ref, o_ref):
    c = pl.program_id(1)
    src = jnp.where(c == 0, meta_ref[...], x_ref[...])
    y = _ln(src, g_ref[...], b_ref[...])
    keep = jnp.logical_or(c > 0, _row_iota(Q) >= PAD)
    o_ref[...] = jnp.where(keep, y, 0.0)


def _embed(x, meta_pad, g, b):
    bsz, t, d = x.shape
    nc = t // Q + 1
    return pl.pallas_call(
        _embed_kernel,
        grid=(bsz, nc),
        in_specs=[
            pl.BlockSpec((None, Q, d), lambda i, c: (i, jnp.maximum(c - 1, 0), 0)),
            pl.BlockSpec((Q, d), lambda i, c: (0, 0)),
            pl.BlockSpec((1, d), lambda i, c: (0, 0)),
            pl.BlockSpec((1, d), lambda i, c: (0, 0)),
        ],
        out_specs=pl.BlockSpec((None, Q, d), lambda i, c: (i, c, 0)),
        out_shape=jax.ShapeDtypeStruct((bsz, nc * Q, d), F32),
        compiler_params=_cparams("parallel", "parallel"),
        name="embed_ln",
    )(x, meta_pad, g, b)


def _head_onehot(offset):
    r = lax.broadcasted_iota(jnp.int32, (LANES, SSD_INNER), 0)
    c = lax.broadcasted_iota(jnp.int32, (LANES, SSD_INNER), 1)
    return jnp.where(r - offset == c // SSD_HEAD_DIM, 1.0, 0.0).astype(BF16)


def _pair_rhs(x_pair):
    lane = lax.broadcasted_iota(jnp.int32, x_pair.shape, 1)
    lo = jnp.where(lane < SSD_HEAD_DIM, x_pair, 0.0)
    hi = jnp.where(lane >= SSD_HEAD_DIM, x_pair, 0.0)
    return jnp.concatenate([lo, hi], axis=0)


def _ssd_forward(act, dt_raw, dtb, alog, first, sf_scr):
    xs = act[:, :SSD_INNER]
    bm = act[:, SSD_INNER:SSD_INNER + BC_DIM]
    cm = act[:, SSD_INNER + BC_DIM:]
    row = _row_iota(Q)
    lane = lax.broadcasted_iota(jnp.int32, (1, LANES), 1)
    dt = _softplus(dt_raw + dtb)
    dt = jnp.where(jnp.logical_or(jnp.logical_not(first), row >= PAD), dt, 0.0)
    a = dt * (-jnp.exp(alog))
    ti = lax.broadcasted_iota(jnp.int32, (Q, Q), 0)
    si = lax.broadcasted_iota(jnp.int32, (Q, Q), 1)
    lower = jnp.where(si <= ti, 1.0, 0.0).astype(BF16)
    upper = jnp.where(si >= ti, 1.0, 0.0).astype(BF16)
    cs = jnp.where(lane < SSD_HEADS, _split_dot(a, lower, 3, True), _split_dot(a, upper, 3, True))
    edge = jnp.where(lane < SSD_HEADS, cs[Q - 1:Q, :], cs[0:1, :])
    wsrc = dt * jnp.exp(edge - cs)
    ecs = jnp.exp(cs)
    cs_t = cs.T
    dt_t = dt.T

    lt = si < ti
    eq = si == ti
    sf = sf_scr[...]
    sf_b = sf.astype(BF16)
    xs_b = xs.astype(BF16)
    ys = []
    for j in range(SSD_HEADS // 2):
        g = (2 * j) // (SSD_HEADS // SSD_GROUPS)
        c_g = cm[:, g * SSD_STATE:(g + 1) * SSD_STATE]
        b_g = bm[:, g * SSD_STATE:(g + 1) * SSD_STATE]
        gmat = _dot_nt(c_g, b_g)
        lhs = []
        for h in (2 * j, 2 * j + 1):
            hb = SSD_HEADS + h
            e = jnp.where(lt, cs[:, h:h + 1] - cs_t[h:h + 1, :], cs[:, hb:hb + 1] - cs_t[hb:hb + 1, :])
            dsel = (jnp.where(lt, dt_t[h:h + 1, :], dt_t[hb:hb + 1, :])
                    + jnp.where(eq, dt_t[h:h + 1, :], 0.0))
            lhs.append((gmat * jnp.exp(e) * dsel).astype(BF16))
        for h in (2 * j, 2 * j + 1):
            lhs.append((c_g * ecs[:, h:h + 1]).astype(BF16))
        lhs = jnp.concatenate(lhs, axis=1)
        sl = slice(2 * j * SSD_HEAD_DIM, (2 * j + 2) * SSD_HEAD_DIM)
        rhs = jnp.concatenate([_pair_rhs(xs_b[:, sl]), _pair_rhs(sf_b[:, sl])], axis=0)
        ys.append(jnp.dot(lhs, rhs, preferred_element_type=F32))
    y_part = jnp.concatenate(ys, axis=1)

    wf = _split_dot(wsrc, _head_onehot(0), 2)
    wb = _split_dot(wsrc, _head_onehot(SSD_HEADS), 2)
    dec = _split_dot(jnp.broadcast_to(ecs[Q - 1:Q, :], (8, LANES)), _head_onehot(0), 2)[0:1, :]
    xf = (xs * wf).astype(BF16)
    xb = (xs * wb).astype(BF16)
    st_f = []
    st_b = []
    for g in range(SSD_GROUPS):
        b_gt = bm[:, g * SSD_STATE:(g + 1) * SSD_STATE].T.astype(BF16)
        sl = slice(g * 256, (g + 1) * 256)
        st_f.append(jnp.dot(b_gt, xf[:, sl], preferred_element_type=F32))
        st_b.append(jnp.dot(b_gt, xb[:, sl], preferred_element_type=F32))
    sf_scr[...] = sf * dec + jnp.concatenate(st_f, axis=1)
    return y_part, jnp.concatenate(st_b, axis=1), ecs


def _ssd_reverse(cm, ecs, sb_chunk, sb_scr):
    sb = sb_scr[...]
    sb_b = sb.astype(BF16)
    ys = []
    for j in range(SSD_HEADS // 2):
        g = (2 * j) // (SSD_HEADS // SSD_GROUPS)
        c_g = cm[:, g * SSD_STATE:(g + 1) * SSD_STATE]
        lhs = [(c_g * ecs[:, SSD_HEADS + h:SSD_HEADS + h + 1]).astype(BF16) for h in (2 * j, 2 * j + 1)]
        sl = slice(2 * j * SSD_HEAD_DIM, (2 * j + 2) * SSD_HEAD_DIM)
        ys.append(jnp.dot(jnp.concatenate(lhs, axis=1), _pair_rhs(sb_b[:, sl]), preferred_element_type=F32))
    dec = _split_dot(jnp.broadcast_to(ecs[0:1, :], (8, LANES)), _head_onehot(SSD_HEADS), 2)[0:1, :]
    sb_scr[...] = sb * dec + sb_chunk
    return jnp.concatenate(ys, axis=1)


def _lru_gates(xc, wg_ref, brg, big, lam):
    xb = xc.astype(BF16)
    pre = []
    for j in range(LRU_WIDTH // 256):
        pre.append(jnp.dot(xb[:, j * 256:(j + 1) * 256], wg_ref[j], preferred_element_type=F32))
    r = _sigmoid(jnp.concatenate([p[:, :256] for p in pre], axis=1) + brg)
    i = _sigmoid(jnp.concatenate([p[:, 256:] for p in pre], axis=1) + big)
    log_a = (-LRU_C) * r * _softplus(-lam)
    a = jnp.exp(log_a)
    u = jnp.sqrt(-jnp.tanh(log_a) * (a * a + 1.0)) * (i * xc)
    return a, u


LRU_SEG = Q // SUBLANES


def _lru_scan(a, u, p_scr, o_scr, h_scr, reverse):
    order = range(LRU_SEG - 1, -1, -1) if reverse else range(LRU_SEG)
    h = jnp.zeros((SUBLANES, a.shape[1]), F32)
    p = jnp.ones((SUBLANES, a.shape[1]), F32)
    for j in order:
        rows = slice(j * SUBLANES, (j + 1) * SUBLANES)
        h = a[rows] * h + u[rows]
        p = a[rows] * p
        o_scr[rows, :] = h
        p_scr[rows, :] = p
    carry = h_scr[...]
    enter = [None] * SUBLANES
    for s in (range(SUBLANES - 1, -1, -1) if reverse else range(SUBLANES)):
        enter[s] = carry
        carry = h[s:s + 1, :] + p[s:s + 1, :] * carry
    h_scr[...] = carry
    enter = jnp.concatenate(enter, axis=0)
    for j in order:
        rows = slice(j * SUBLANES, (j + 1) * SUBLANES)
        o_scr[rows, :] = o_scr[rows, :] + p_scr[rows, :] * enter


def _interleaved_rows(slab_scr, first, count):
    groups = [jnp.concatenate([slab_scr[s, pl.ds(first + g, SUBLANES, stride=LRU_SEG), :]
                               for s in range(slab_scr.shape[0])], axis=1) for g in range(count)]
    return jnp.concatenate(groups, axis=0)


def _to_slabs(x, slab_scr):
    for s in range(slab_scr.shape[0]):
        slab_scr[s] = x[:, s * LANES:(s + 1) * LANES]


def _fwd_kernel(hp_ref, hc_ref, hn_ref, wx_ref, wl_ref, scw_ref, scb_ref, dtb_ref, alog_ref,
                lcw_ref, lcb_ref, wg_ref, brg_ref, big_ref, lam_ref,
                xs_o, cm_o, yp_o, sb_o, aux_o, xc_o, hf_o,
                sf_scr, ext_scr, p_scr, hrow_scr, h_scr):
    c = pl.program_id(1)
    nc = pl.num_programs(1)
    first = c == 0

    @pl.when(first)
    def _():
        sf_scr[...] = jnp.zeros_like(sf_scr)
        h_scr[...] = jnp.zeros_like(h_scr)

    nxt = hn_ref[...] * jnp.where(c < nc - 1, 1.0, 0.0)
    ext = jnp.concatenate([hp_ref[...], hc_ref[...], nxt], axis=0)
    _to_slabs(ext, ext_scr)
    pe = jnp.dot(ext.astype(BF16), wx_ref[...], preferred_element_type=F32)

    act = _silu(_dwconv4(pe[:, :XBC_DIM], scw_ref[...], scb_ref[...]))
    dt_raw = pe[HALO:HALO + Q, XBC_DIM:]
    y_part, st_b, ecs = _ssd_forward(act, dt_raw, dtb_ref[...], alog_ref[...], first, sf_scr)
    xs_o[...] = act[:, :SSD_INNER].astype(xs_o.dtype)
    cm_o[...] = act[:, SSD_INNER + BC_DIM:].astype(cm_o.dtype)
    yp_o[...] = y_part.astype(yp_o.dtype)
    sb_o[...] = st_b.astype(sb_o.dtype)
    aux_o[...] = ecs

    xl = _interleaved_rows(ext_scr, HALO - 2, LRU_SEG + 3).astype(BF16)
    pl_ = jnp.dot(xl, wl_ref[...], preferred_element_type=F32)
    lcw = lcw_ref[...]
    xc = lcb_ref[...]
    for k in range(4):
        xc = xc + pl_[k * SUBLANES:k * SUBLANES + Q] * lcw[k:k + 1, :]
    xc_o[...] = xc.astype(xc_o.dtype)
    a, u = _lru_gates(xc, wg_ref, brg_ref[...], big_ref[...], lam_ref[...])
    assert PAD % LRU_SEG == 0
    live = jnp.logical_or(jnp.logical_not(first), (_row_iota(Q) & (SUBLANES - 1)) >= PAD // LRU_SEG)
    _lru_scan(a, jnp.where(live, u, 0.0), p_scr, hrow_scr, h_scr, reverse=False)
    hf_o[...] = hrow_scr[...].astype(hf_o.dtype)


def _fwd(h, lw):
    bsz, lp, d = h.shape
    nc = lp // Q
    nh = lp // HALO
    per = Q // HALO
    w = LRU_WIDTH
    seq = lambda n: pl.BlockSpec((None, Q, n), lambda i, c: (i, c, 0))
    return pl.pallas_call(
        _fwd_kernel,
        grid=(bsz, nc),
        in_specs=[
            pl.BlockSpec((None, HALO, d), lambda i, c: (i, jnp.maximum(c * per - 1, 0), 0)),
            seq(d),
            pl.BlockSpec((None, HALO, d), lambda i, c: (i, jnp.minimum((c + 1) * per, nh - 1), 0)),
            _const_spec((d, XBC_DIM + LANES)), _const_spec((d, w)),
            _const_spec((4, XBC_DIM)), _const_spec((1, XBC_DIM)), _const_spec((1, LANES)), _const_spec((1, LANES)),
            _const_spec((4, w)), _const_spec((1, w)), _const_spec((w // 256, 256, 512)),
            _const_spec((1, w)), _const_spec((1, w)), _const_spec((1, w)),
        ],
        out_specs=[
            seq(SSD_INNER), seq(BC_DIM), seq(SSD_INNER),
            pl.BlockSpec((None, None, SSD_STATE, SSD_INNER), lambda i, c: (i, c, 0, 0)),
            seq(LANES), seq(w), seq(w),
        ],
        out_shape=[
            jax.ShapeDtypeStruct((bsz, lp, SSD_INNER), ACT_DT),
            jax.ShapeDtypeStruct((bsz, lp, BC_DIM), ACT_DT),
            jax.ShapeDtypeStruct((bsz, lp, SSD_INNER), ACT_DT),
            jax.ShapeDtypeStruct((bsz, nc, SSD_STATE, SSD_INNER), ACT_DT),
            jax.ShapeDtypeStruct((bsz, lp, LANES), F32),
            jax.ShapeDtypeStruct((bsz, lp, w), ACT_DT),
            jax.ShapeDtypeStruct((bsz, lp, w), ACT_DT),
        ],
        scratch_shapes=[pltpu.VMEM((SSD_STATE, SSD_INNER), F32),
                        pltpu.VMEM((d // LANES, Q + 2 * HALO, LANES), F32),
                        pltpu.VMEM((Q, w), F32), pltpu.VMEM((Q, w), F32), pltpu.VMEM((1, w), F32)],
        compiler_params=_cparams("parallel", "arbitrary"),
        name="fwd_sweep",
    )(h, h, h, lw["w_xdt"], lw["w_lx"], lw["ssd_conv_w"], lw["ssd_conv_b"], lw["dt_bias"], lw["a_log"],
      lw["lru_conv_w"], lw["lru_conv_b"], lw["wg"][0], lw["brg"][0], lw["big"][0], lw["lam"][0])


def _rev_kernel(h_ref, xs_ref, cm_ref, yp_ref, sb_ref, aux_ref, xc_ref, hf_ref,
                wr_ref, wlg_ref, dskip_ref, ng_ref, wso_ref, wg_ref, brg_ref, big_ref, lam_ref, wlo_ref,
                wo_ref, g_ref, b_ref, o_ref,
                sb_scr, slab_scr, p_scr, hrow_scr, h_scr):
    k = pl.program_id(1)
    nc = pl.num_programs(1)

    @pl.when(k == 0)
    def _():
        sb_scr[...] = jnp.zeros_like(sb_scr)
        h_scr[...] = jnp.zeros_like(h_scr)

    h = h_ref[...]
    pr = _dot(h, wr_ref[...])
    z = pr[:, :SSD_INNER]
    mg = pr[:, SSD_INNER:]

    y_off = _ssd_reverse(cm_ref[...].astype(F32), aux_ref[...], sb_ref[...].astype(F32), sb_scr)
    y = yp_ref[...].astype(F32) + y_off + xs_ref[...].astype(F32) * dskip_ref[...]
    y = y * _silu(z)
    y = y * lax.rsqrt(jnp.mean(y * y, -1, keepdims=True) + RMS_EPS) * ng_ref[...]
    y_s = _dot(y, wso_ref[...])

    _to_slabs(h, slab_scr)
    lg = _dot(_interleaved_rows(slab_scr, 0, LRU_SEG), wlg_ref[...])
    a, u = _lru_gates(xc_ref[...].astype(F32), wg_ref, brg_ref[...], big_ref[...], lam_ref[...])
    _lru_scan(a, u, p_scr, hrow_scr, h_scr, reverse=True)
    y_li = _dot((hf_ref[...].astype(F32) + hrow_scr[...]) * _gelu(lg), wlo_ref[...])
    _to_slabs(y_li, slab_scr)
    per_seg = LRU_SEG // SUBLANES
    y_l = jnp.concatenate(
        [jnp.concatenate([slab_scr[s, pl.ds((i % per_seg) * SUBLANES * SUBLANES + i // per_seg, SUBLANES,
                                            stride=SUBLANES), :] for s in range(slab_scr.shape[0])], axis=1)
         for i in range(Q // SUBLANES)], axis=0)

    mix = _sigmoid(mg[:, :D_MODEL]) * y_s + _sigmoid(mg[:, D_MODEL:]) * y_l
    out = _ln(ALPHA * h + _dot(mix, wo_ref[...]), g_ref[...], b_ref[...])
    o_ref[...] = jnp.where(jnp.logical_or(k < nc - 1, _row_iota(Q) >= PAD), out, 0.0)


def _rev(h, xs, cm, yp, sb, aux, xc, hf, lw):
    bsz, lp, d = h.shape
    nc = lp // Q
    w = LRU_WIDTH
    seq = lambda n: pl.BlockSpec((None, Q, n), lambda i, c: (i, nc - 1 - c, 0))
    return pl.pallas_call(
        _rev_kernel,
        grid=(bsz, nc),
        in_specs=[
            seq(d), seq(SSD_INNER), seq(BC_DIM), seq(SSD_INNER),
            pl.BlockSpec((None, None, SSD_STATE, SSD_INNER), lambda i, c: (i, nc - 1 - c, 0, 0)),
            seq(LANES), seq(w), seq(w),
            _const_spec((d, SSD_INNER + 2 * d)), _const_spec((d, w)),
            _const_spec((1, SSD_INNER)), _const_spec((1, SSD_INNER)), _const_spec((SSD_INNER, d)),
            _const_spec((w // 256, 256, 512)), _const_spec((1, w)), _const_spec((1, w)), _const_spec((1, w)),
            _const_spec((w, d)), _const_spec((d, d)), _const_spec((1, d)), _const_spec((1, d)),
        ],
        out_specs=seq(d),
        out_shape=jax.ShapeDtypeStruct((bsz, lp, d), F32),
        scratch_shapes=[pltpu.VMEM((SSD_STATE, SSD_INNER), F32), pltpu.VMEM((d // LANES, Q, LANES), F32),
                        pltpu.VMEM((Q, w), F32), pltpu.VMEM((Q, w), F32), pltpu.VMEM((1, w), F32)],
        compiler_params=_cparams("parallel", "arbitrary"),
        name="rev_sweep",
    )(h, xs, cm, yp, sb, aux, xc, hf, lw["w_zm"], lw["w_lg"], lw["dskip"], lw["norm_g"], lw["w_ssd_out"],
      lw["wg"][1], lw["brg"][1], lw["big"][1], lw["lam"][1], lw["w_lru_out"], lw["w_o"],
      lw["ln1_g"], lw["ln1_b"])


def _ffn_kernel(hp_ref, hc_ref, hn_ref, wu_ref, cw_ref, cb_ref, wd_ref, g_ref, b_ref, o_ref,
                ext_scr, out_scr):
    c = pl.program_id(1)
    nc = pl.num_programs(1)
    tf = hc_ref.shape[0]
    nseg = tf // SEG
    slabs = D_MODEL // LANES
    keep_next = jnp.where(c < nc - 1, 1.0, 0.0)
    for s in range(slabs):
        ls = slice(s * LANES, (s + 1) * LANES)
        ext_scr[s, 0:HALO, :] = hp_ref[:, ls]
        ext_scr[s, HALO:HALO + tf, :] = hc_ref[:, ls]
        ext_scr[s, HALO + tf:, :] = hn_ref[:, ls] * keep_next

    def group(g):
        halves = [jnp.concatenate([ext_scr[s, pl.ds(HALO + g + k * SUBLANES * nseg, SUBLANES, stride=nseg), :]
                                   for s in range(slabs)], axis=1) for k in range(SEG // SUBLANES)]
        return jnp.concatenate(halves, axis=0)
    xperm = jnp.concatenate([group(g) for g in range(-1, nseg + 1)], axis=0)
    xb = xperm.astype(BF16)

    def conv3(up, col):
        cw = cw_ref[:, pl.ds(col, FFN_COLS)].astype(BF16)
        acc = cb_ref[:, pl.ds(col, FFN_COLS)].astype(BF16) + up[0:tf] * cw[0:1, :]
        acc = acc + up[SEG:SEG + tf] * cw[1:2, :]
        return acc + up[2 * SEG:2 * SEG + tf] * cw[2:3, :]

    f = None
    for j in range(D_FF // FFN_COLS):
        cg, cv = j * FFN_COLS, D_FF + j * FFN_COLS
        ug = conv3(jnp.dot(xb, wu_ref[:, pl.ds(cg, FFN_COLS)], preferred_element_type=F32).astype(BF16), cg)
        uv = conv3(jnp.dot(xb, wu_ref[:, pl.ds(cv, FFN_COLS)], preferred_element_type=F32).astype(BF16), cv)
        part = jnp.dot(_gelu(ug) * uv, wd_ref[pl.ds(cg, FFN_COLS), :], preferred_element_type=F32)
        f = part if f is None else f + part
    y = _ln(ALPHA * xperm[SEG:SEG + tf] + f, g_ref[...], b_ref[...])
    p = _row_iota(tf)
    t = c * tf + (p & (SEG - 1)) * nseg + (p >> SEG_SHIFT)
    y = jnp.where(t >= PAD, y, 0.0)
    for j in range(nseg):
        for k in range(SEG // SUBLANES):
            r0 = j * SEG + k * SUBLANES
            for s in range(slabs):
                out_scr[s, pl.ds(j + k * SUBLANES * nseg, SUBLANES, stride=nseg), :] = (
                    y[r0:r0 + SUBLANES, s * LANES:(s + 1) * LANES])
    for s in range(slabs):
        o_ref[:, s * LANES:(s + 1) * LANES] = out_scr[s]


def _ffn_tile(lp):
    for parts in (4, 8, 2, 1):
        if lp % (parts * SEG) == 0 and lp // parts <= 640:
            return lp // parts
    return Q


def _ffn(h, lw):
    bsz, lp, d = h.shape
    tf = _ffn_tile(lp)
    nh = lp // HALO
    per = tf // HALO
    seq = pl.BlockSpec((None, tf, d), lambda i, c: (i, c, 0))
    return pl.pallas_call(
        _ffn_kernel,
        grid=(bsz, lp // tf),
        in_specs=[
            pl.BlockSpec((None, HALO, d), lambda i, c: (i, jnp.maximum(c * per - 1, 0), 0)),
            seq,
            pl.BlockSpec((None, HALO, d), lambda i, c: (i, jnp.minimum((c + 1) * per, nh - 1), 0)),
            _const_spec((d, 2 * D_FF)), _const_spec((3, 2 * D_FF)), _const_spec((1, 2 * D_FF)),
            _const_spec((D_FF, d)), _const_spec((1, d)), _const_spec((1, d)),
        ],
        out_specs=seq,
        out_shape=jax.ShapeDtypeStruct((bsz, lp, d), F32),
        scratch_shapes=[pltpu.VMEM((d // LANES, tf + 2 * HALO, LANES), F32),
                        pltpu.VMEM((d // LANES, tf, LANES), F32)],
        compiler_params=_cparams("parallel", "parallel"),
        name="ffn_ln",
    )(h, h, h, lw["w_up"], lw["ffn_conv_w"], lw["ffn_conv_b"], lw["w_down"], lw["ln2_g"], lw["ln2_b"])


def _pad_lanes(v):
    return jnp.pad(v.reshape(1, -1).astype(F32), ((0, 0), (0, LANES - v.size)))


def _gate_tiles(w_rg, w_ig):
    def tiles(w):
        w4 = w.reshape(LRU_HEADS // 4, 4, LRU_BLOCK, LRU_BLOCK)
        eye = jnp.eye(4, dtype=w.dtype)
        return jnp.einsum("thij,hk->thikj", w4, eye).reshape(LRU_HEADS // 4, 256, 256)
    return jnp.concatenate([tiles(w_rg), tiles(w_ig)], axis=-1).astype(BF16)


def _prep_layer(p, li):
    w_in = p["w_in"][li]
    row = lambda v: v.reshape(1, -1).astype(F32)
    w_dt = jnp.pad(w_in[:, S2:S3], ((0, 0), (0, LANES - (S3 - S2))))
    return dict(
        w_xdt=jnp.concatenate([w_in[:, S1:S2], w_dt], axis=1).astype(BF16),
        w_lx=w_in[:, S4:S5].astype(BF16),
        w_zm=jnp.concatenate([w_in[:, :S1], w_in[:, S5:]], axis=1).astype(BF16),
        w_lg=w_in[:, S3:S4].astype(BF16),
        ssd_conv_w=p["ssd_conv_w"][li], ssd_conv_b=row(p["ssd_conv_b"][li]),
        dt_bias=_pad_lanes(p["ssd_dt_bias"][li]), a_log=_pad_lanes(p["ssd_a_log"][li]),
        dskip=row(jnp.repeat(p["ssd_d"][li], SSD_HEAD_DIM)), norm_g=row(p["ssd_norm_g"][li]),
        w_ssd_out=p["w_ssd_out"][li].astype(BF16),
        lru_conv_w=p["lru_conv_w"][li], lru_conv_b=row(p["lru_conv_b"][li]),
        wg=[_gate_tiles(p["lru_w_rg"][li, d], p["lru_w_ig"][li, d]) for d in range(2)],
        brg=[row(p["lru_b_rg"][li, d]) for d in range(2)],
        big=[row(p["lru_b_ig"][li, d]) for d in range(2)],
        lam=[row(p["lru_lambda"][li, d]) for d in range(2)],
        w_lru_out=p["w_lru_out"][li].astype(BF16), w_o=p["w_o"][li].astype(BF16),
        ln1_g=row(p["ln1_g"][li]), ln1_b=row(p["ln1_b"][li]),
        w_up=p["w_up"][li].astype(BF16), ffn_conv_w=p["ffn_conv_w"][li],
        ffn_conv_b=row(p["ffn_conv_b"][li]), w_down=p["w_down"][li].astype(BF16),
        ln2_g=row(p["ln2_g"][li]), ln2_b=row(p["ln2_b"][li]),
    )


def _layer(h, lw):
    xs, cm, yp, sb, aux, xc, hf = _fwd(h, lw)
    h = _rev(h, xs, cm, yp, sb, aux, xc, hf, lw)
    return _ffn(h, lw)


def _run(x, meta_pad, ln_g, ln_b, layers):
    h = _embed(x, meta_pad, ln_g, ln_b)
    for lw in layers:
        h = _layer(h, lw)
    return h[:, Q:]


def kernel(x_prompt, x_sample, meta, ln_in_g, ln_in_b, w_in, ssd_conv_w, ssd_conv_b, ssd_dt_bias, ssd_a_log, ssd_d, ssd_norm_g, w_ssd_out, lru_conv_w, lru_conv_b, lru_w_rg, lru_b_rg, lru_w_ig, lru_b_ig, lru_lambda, w_lru_out, w_o, ln1_g, ln1_b, w_up, ffn_conv_w, ffn_conv_b, w_down, ln2_g, ln2_b):
    p = dict(w_in=w_in, ssd_conv_w=ssd_conv_w, ssd_conv_b=ssd_conv_b, ssd_dt_bias=ssd_dt_bias,
             ssd_a_log=ssd_a_log, ssd_d=ssd_d, ssd_norm_g=ssd_norm_g, w_ssd_out=w_ssd_out,
             lru_conv_w=lru_conv_w, lru_conv_b=lru_conv_b, lru_w_rg=lru_w_rg, lru_b_rg=lru_b_rg,
             lru_w_ig=lru_w_ig, lru_b_ig=lru_b_ig, lru_lambda=lru_lambda, w_lru_out=w_lru_out,
             w_o=w_o, ln1_g=ln1_g, ln1_b=ln1_b, w_up=w_up, ffn_conv_w=ffn_conv_w,
             ffn_conv_b=ffn_conv_b, w_down=w_down, ln2_g=ln2_g, ln2_b=ln2_b)
    layers = [_prep_layer(p, li) for li in range(w_in.shape[0])]
    meta_pad = jnp.pad(meta.astype(F32), ((PAD, 0), (0, 0)))
    g = ln_in_g.reshape(1, -1)
    b = ln_in_b.reshape(1, -1)
    return (_run(x_prompt, meta_pad, g, b, layers), _run(x_sample, meta_pad, g, b, layers))
```

```python
import math

import jax
import jax.numpy as jnp
from jax import lax
from jax.experimental import pallas as pl
from jax.experimental.pallas import tpu as pltpu

F32 = jnp.float32
BF16 = jnp.bfloat16

D_MODEL = 1024
DEPTH = 4
N_META = 16
SSD_HEADS = 16
SSD_HEAD_DIM = 64
SSD_INNER = SSD_HEADS * SSD_HEAD_DIM
SSD_GROUPS = 4
SSD_STATE = 128
BC_DIM = SSD_GROUPS * SSD_STATE
XBC_DIM = SSD_INNER + 2 * BC_DIM
LRU_WIDTH = 1024
LRU_HEADS = 16
LRU_BLOCK = LRU_WIDTH // LRU_HEADS
LRU_C = 8.0
D_FF = 3 * D_MODEL
S1 = SSD_INNER
S2 = S1 + XBC_DIM
S3 = S2 + 2 * SSD_HEADS
S4 = S3 + LRU_WIDTH
S5 = S4 + LRU_WIDTH
ALPHA = (2 * DEPTH) ** 0.25
LN_EPS = 1e-5
RMS_EPS = 1e-6

Q = 128
PAD = Q - N_META
HALO = 8
LANES = 128
SUBLANES = 8
LRU_SEG = Q // SUBLANES
SEG_SHIFT = 4
SEG = 1 << SEG_SHIFT
FFN_COLS = 512
ACT_DT = BF16
SEQ_PER_STEP = 2
VMEM_LIMIT = 56 * 1024 * 1024


def _cparams(*sem):
    return pltpu.CompilerParams(dimension_semantics=sem, vmem_limit_bytes=VMEM_LIMIT)


def _const_spec(shape):
    zeros = (0,) * len(shape)
    return pl.BlockSpec(shape, lambda i, c: zeros, pipeline_mode=pl.Buffered(1))


def _dot(a, b):
    return jnp.dot(a.astype(BF16), b.astype(BF16), preferred_element_type=F32)


def _dot_nt(a, b):
    return lax.dot_general(a.astype(BF16), b.astype(BF16), (((1,), (1,)), ((), ())),
                           preferred_element_type=F32)


def _split_dot(v, e, terms, e_left=False):
    acc = None
    r = v
    for _ in range(terms):
        hi = r.astype(BF16)
        part = (jnp.dot(e, hi, preferred_element_type=F32) if e_left
                else jnp.dot(hi, e, preferred_element_type=F32))
        acc = part if acc is None else acc + part
        r = r - hi.astype(F32)
    return acc


def _ln(x, g, b):
    mu = jnp.mean(x, -1, keepdims=True)
    xc = x - mu
    var = jnp.mean(xc * xc, -1, keepdims=True)
    return xc * lax.rsqrt(var + LN_EPS) * g + b


def _sigmoid(x):
    return 1.0 / (1.0 + jnp.exp(-x))


def _silu(x):
    return x * _sigmoid(x)


def _gelu(x):
    c = math.sqrt(2.0 / math.pi)
    return x * (0.5 * (1.0 + jnp.tanh(c * (x + 0.044715 * (x * x * x)))))


def _log1p(e):
    u = 1.0 + e
    return jnp.where(u == 1.0, e, jnp.log(u) * (e / (u - 1.0)))


def _softplus(x):
    return jnp.maximum(x, 0.0) + _log1p(jnp.exp(-jnp.abs(x)))


def _row_iota(n):
    return lax.broadcasted_iota(jnp.int32, (n, 1), 0)


def _row_time(p):
    return (p & (SUBLANES - 1)) * LRU_SEG + (p >> 3)


def _embed_kernel(x_ref, meta_ref, g_ref, b_ref, o_ref):
    c = pl.program_id(1)
    src = jnp.where(c == 0, meta_ref[...], x_ref[...])
    y = _ln(src, g_ref[...], b_ref[...])
    keep = jnp.logical_or(c > 0, _row_iota(Q) >= PAD)
    o_ref[...] = jnp.where(keep, y, 0.0)


def _embed(x, meta_pad, g, b):
    bsz, t, d = x.shape
    nc = t // Q + 1
    return pl.pallas_call(
        _embed_kernel,
        grid=(bsz, nc),
        in_specs=[
            pl.BlockSpec((None, Q, d), lambda i, c: (i, jnp.maximum(c - 1, 0), 0)),
            pl.BlockSpec((Q, d), lambda i, c: (0, 0)),
            pl.BlockSpec((1, d), lambda i, c: (0, 0)),
            pl.BlockSpec((1, d), lambda i, c: (0, 0)),
        ],
        out_specs=pl.BlockSpec((None, Q, d), lambda i, c: (i, c, 0)),
        out_shape=jax.ShapeDtypeStruct((bsz, nc * Q, d), F32),
        compiler_params=_cparams("parallel", "parallel"),
        name="embed_ln",
    )(x, meta_pad, g, b)


def _head_onehot(offset):
    r = lax.broadcasted_iota(jnp.int32, (LANES, SSD_INNER), 0)
    c = lax.broadcasted_iota(jnp.int32, (LANES, SSD_INNER), 1)
    return jnp.where(r - offset == c // SSD_HEAD_DIM, 1.0, 0.0).astype(BF16)


def _pair_rhs(x_pair):
    lane = lax.broadcasted_iota(jnp.int32, x_pair.shape, 1)
    lo = jnp.where(lane < SSD_HEAD_DIM, x_pair, 0.0)
    hi = jnp.where(lane >= SSD_HEAD_DIM, x_pair, 0.0)
    return jnp.concatenate([lo, hi], axis=0)


def _ssd_forward(act, dt_raw, dtb, alog, first, sf_scr):
    xs = act[:, :SSD_INNER]
    bm = act[:, SSD_INNER:SSD_INNER + BC_DIM]
    cm = act[:, SSD_INNER + BC_DIM:]
    lane = lax.broadcasted_iota(jnp.int32, (1, LANES), 1)
    dt = _softplus(dt_raw + dtb)
    dt = jnp.where(jnp.logical_or(jnp.logical_not(first), _row_time(_row_iota(Q)) >= PAD), dt, 0.0)
    a = dt * (-jnp.exp(alog))
    ti = _row_time(lax.broadcasted_iota(jnp.int32, (Q, Q), 0))
    si = _row_time(lax.broadcasted_iota(jnp.int32, (Q, Q), 1))
    lower = jnp.where(si <= ti, 1.0, 0.0).astype(BF16)
    upper = jnp.where(si >= ti, 1.0, 0.0).astype(BF16)
    cs = jnp.where(lane < SSD_HEADS, _split_dot(a, lower, 3, True), _split_dot(a, upper, 3, True))
    edge = jnp.where(lane < SSD_HEADS, cs[Q - 1:Q, :], cs[0:1, :])
    wsrc = dt * jnp.exp(edge - cs)
    ecs = jnp.exp(cs)
    cs_t = cs.T
    dt_t = dt.T

    lt = si < ti
    eq = si == ti
    sf = sf_scr[...]
    sf_b = sf.astype(BF16)
    xs_b = xs.astype(BF16)
    ys = []
    for j in range(SSD_HEADS // 2):
        g = (2 * j) // (SSD_HEADS // SSD_GROUPS)
        c_g = cm[:, g * SSD_STATE:(g + 1) * SSD_STATE]
        b_g = bm[:, g * SSD_STATE:(g + 1) * SSD_STATE]
        gmat = _dot_nt(c_g, b_g)
        lhs = []
        for h in (2 * j, 2 * j + 1):
            hb = SSD_HEADS + h
            e = jnp.where(lt, cs[:, h:h + 1] - cs_t[h:h + 1, :], cs[:, hb:hb + 1] - cs_t[hb:hb + 1, :])
            dsel = (jnp.where(lt, dt_t[h:h + 1, :], dt_t[hb:hb + 1, :])
                    + jnp.where(eq, dt_t[h:h + 1, :], 0.0))
            lhs.append((gmat * jnp.exp(e) * dsel).astype(BF16))
        for h in (2 * j, 2 * j + 1):
            lhs.append((c_g * ecs[:, h:h + 1]).astype(BF16))
        lhs = jnp.concatenate(lhs, axis=1)
        sl = slice(2 * j * SSD_HEAD_DIM, (2 * j + 2) * SSD_HEAD_DIM)
        rhs = jnp.concatenate([_pair_rhs(xs_b[:, sl]), _pair_rhs(sf_b[:, sl])], axis=0)
        ys.append(jnp.dot(lhs, rhs, preferred_element_type=F32))
    y_part = jnp.concatenate(ys, axis=1)

    wf = _split_dot(wsrc, _head_onehot(0), 2)
    wb = _split_dot(wsrc, _head_onehot(SSD_HEADS), 2)
    dec = _split_dot(jnp.broadcast_to(ecs[Q - 1:Q, :], (8, LANES)), _head_onehot(0), 2)[0:1, :]
    xf = (xs * wf).astype(BF16)
    xb = (xs * wb).astype(BF16)
    st_f = []
    st_b = []
    for g in range(SSD_GROUPS):
        b_gt = bm[:, g * SSD_STATE:(g + 1) * SSD_STATE].T.astype(BF16)
        sl = slice(g * 256, (g + 1) * 256)
        st_f.append(jnp.dot(b_gt, xf[:, sl], preferred_element_type=F32))
        st_b.append(jnp.dot(b_gt, xb[:, sl], preferred_element_type=F32))
    sf_scr[...] = sf * dec + jnp.concatenate(st_f, axis=1)
    return y_part, jnp.concatenate(st_b, axis=1), ecs


def _ssd_reverse(cm, ecs, sb_chunk, sb_scr):
    sb = sb_scr[...]
    sb_b = sb.astype(BF16)
    ys = []
    for j in range(SSD_HEADS // 2):
        g = (2 * j) // (SSD_HEADS // SSD_GROUPS)
        c_g = cm[:, g * SSD_STATE:(g + 1) * SSD_STATE]
        lhs = [(c_g * ecs[:, SSD_HEADS + h:SSD_HEADS + h + 1]).astype(BF16) for h in (2 * j, 2 * j + 1)]
        sl = slice(2 * j * SSD_HEAD_DIM, (2 * j + 2) * SSD_HEAD_DIM)
        ys.append(jnp.dot(jnp.concatenate(lhs, axis=1), _pair_rhs(sb_b[:, sl]), preferred_element_type=F32))
    dec = _split_dot(jnp.broadcast_to(ecs[0:1, :], (8, LANES)), _head_onehot(SSD_HEADS), 2)[0:1, :]
    sb_scr[...] = sb * dec + sb_chunk
    return jnp.concatenate(ys, axis=1)


def _lru_gates(xc, wg_ref, brg, big, lam):
    xb = xc.astype(BF16)
    pre = []
    for j in range(LRU_WIDTH // 256):
        pre.append(jnp.dot(xb[:, j * 256:(j + 1) * 256], wg_ref[j], preferred_element_type=F32))
    r = _sigmoid(jnp.concatenate([p[:, :256] for p in pre], axis=1) + brg)
    i = _sigmoid(jnp.concatenate([p[:, 256:] for p in pre], axis=1) + big)
    log_a = (-LRU_C) * r * _softplus(-lam)
    a = jnp.exp(log_a)
    u = jnp.sqrt(-jnp.tanh(log_a) * (a * a + 1.0)) * (i * xc)
    return a, u


def _lru_scan(a, u, p_scr, o_scr, h_scr, reverse):
    order = range(LRU_SEG - 1, -1, -1) if reverse else range(LRU_SEG)
    h = jnp.zeros((SUBLANES, a.shape[1]), F32)
    p = jnp.ones((SUBLANES, a.shape[1]), F32)
    for j in order:
        rows = slice(j * SUBLANES, (j + 1) * SUBLANES)
        h = a[rows] * h + u[rows]
        p = a[rows] * p
        o_scr[rows, :] = h
        p_scr[rows, :] = p
    carry = h_scr[...]
    enter = [None] * SUBLANES
    for s in (range(SUBLANES - 1, -1, -1) if reverse else range(SUBLANES)):
        enter[s] = carry
        carry = h[s:s + 1, :] + p[s:s + 1, :] * carry
    h_scr[...] = carry
    enter = jnp.concatenate(enter, axis=0)
    for j in order:
        rows = slice(j * SUBLANES, (j + 1) * SUBLANES)
        o_scr[rows, :] = o_scr[rows, :] + p_scr[rows, :] * enter


def _interleaved_rows(slab_scr, first, count):
    groups = [jnp.concatenate([slab_scr[s, pl.ds(first + g, SUBLANES, stride=LRU_SEG), :]
                               for s in range(slab_scr.shape[0])], axis=1) for g in range(count)]
    return jnp.concatenate(groups, axis=0)


def _to_slabs(x, slab_scr):
    for s in range(slab_scr.shape[0]):
        slab_scr[s] = x[:, s * LANES:(s + 1) * LANES]


def _dwconv4(pe, cw, cb):
    acc = cb
    for k in range(4):
        acc = acc + pe[k * SUBLANES:k * SUBLANES + Q] * cw[k:k + 1, :]
    return acc


PK_XS, PK_CM, PK_YP, PK_XC, PK_HF = 0, SSD_INNER, SSD_INNER + BC_DIM, 2 * SSD_INNER + BC_DIM, 2 * SSD_INNER + BC_DIM + LRU_WIDTH
PK_WIDTH = PK_HF + LRU_WIDTH


def _fwd_kernel(hp_ref, hc_ref, hn_ref, wf_ref, scw_ref, scb_ref, dtb_ref, alog_ref,
                lcw_ref, lcb_ref, wg_ref, brg_ref, big_ref, lam_ref,
                pk_o, sb_o, aux_o,
                sf_scr, ext_scr, p_scr, hrow_scr, h_scr):
    c = pl.program_id(1)
    first = c == 0

    @pl.when(first)
    def _():
        sf_scr[...] = jnp.zeros_like(sf_scr)
        h_scr[...] = jnp.zeros_like(h_scr)

    for i in range(SEQ_PER_STEP):
        _fwd_one(hp_ref.at[i], hc_ref.at[i], hn_ref.at[i], wf_ref, scw_ref, scb_ref, dtb_ref, alog_ref,
                 lcw_ref, lcb_ref, wg_ref, brg_ref, big_ref, lam_ref,
                 pk_o.at[i], sb_o.at[i], aux_o.at[i],
                 sf_scr.at[i], ext_scr.at[i], p_scr.at[i], hrow_scr.at[i], h_scr.at[i])


def _fwd_one(hp_ref, hc_ref, hn_ref, wf_ref, scw_ref, scb_ref, dtb_ref, alog_ref,
             lcw_ref, lcb_ref, wg_ref, brg_ref, big_ref, lam_ref,
             pk_o, sb_o, aux_o,
             sf_scr, ext_scr, p_scr, hrow_scr, h_scr):
    c = pl.program_id(1)
    nc = pl.num_programs(1)
    first = c == 0
    nxt = hn_ref[...] * jnp.where(c < nc - 1, 1.0, 0.0)
    _to_slabs(jnp.concatenate([hp_ref[...], hc_ref[...], nxt], axis=0), ext_scr)
    xl = _interleaved_rows(ext_scr, HALO - 2, LRU_SEG + 3).astype(BF16)
    pe = jnp.dot(xl, wf_ref[...], preferred_element_type=F32)

    act = _silu(_dwconv4(pe[:, :XBC_DIM], scw_ref[...], scb_ref[...]))
    dt_raw = pe[2 * SUBLANES:2 * SUBLANES + Q, XBC_DIM + LRU_WIDTH:]
    y_part, st_b, ecs = _ssd_forward(act, dt_raw, dtb_ref[...], alog_ref[...], first, sf_scr)
    pk_o[:, PK_XS:PK_CM] = act[:, :SSD_INNER].astype(pk_o.dtype)
    pk_o[:, PK_CM:PK_YP] = act[:, SSD_INNER + BC_DIM:].astype(pk_o.dtype)
    pk_o[:, PK_YP:PK_XC] = y_part.astype(pk_o.dtype)
    sb_o[...] = st_b.astype(sb_o.dtype)
    aux_o[...] = ecs

    xc = _dwconv4(pe[:, XBC_DIM:XBC_DIM + LRU_WIDTH], lcw_ref[...], lcb_ref[...])
    pk_o[:, PK_XC:PK_HF] = xc.astype(pk_o.dtype)
    a, u = _lru_gates(xc, wg_ref, brg_ref[...], big_ref[...], lam_ref[...])
    live = jnp.logical_or(jnp.logical_not(first), _row_time(_row_iota(Q)) >= PAD)
    _lru_scan(a, jnp.where(live, u, 0.0), p_scr, hrow_scr, h_scr, reverse=False)
    pk_o[:, PK_HF:] = hrow_scr[...].astype(pk_o.dtype)


def _fwd(h, lw):
    bsz, lp, d = h.shape
    nc = lp // Q
    nh = lp // HALO
    per = Q // HALO
    w = LRU_WIDTH
    nb = SEQ_PER_STEP
    seq = lambda n: pl.BlockSpec((nb, Q, n), lambda i, c: (i, c, 0))
    return pl.pallas_call(
        _fwd_kernel,
        grid=(bsz // nb, nc),
        in_specs=[
            pl.BlockSpec((nb, HALO, d), lambda i, c: (i, jnp.maximum(c * per - 1, 0), 0)),
            seq(d),
            pl.BlockSpec((nb, HALO, d), lambda i, c: (i, jnp.minimum((c + 1) * per, nh - 1), 0)),
            _const_spec((d, XBC_DIM + w + LANES)),
            _const_spec((4, XBC_DIM)), _const_spec((1, XBC_DIM)), _const_spec((1, LANES)), _const_spec((1, LANES)),
            _const_spec((4, w)), _const_spec((1, w)), _const_spec((w // 256, 256, 512)),
            _const_spec((1, w)), _const_spec((1, w)), _const_spec((1, w)),
        ],
        out_specs=[
            seq(PK_WIDTH),
            pl.BlockSpec((nb, None, SSD_STATE, SSD_INNER), lambda i, c: (i, c, 0, 0)),
            seq(LANES),
        ],
        out_shape=[
            jax.ShapeDtypeStruct((bsz, lp, PK_WIDTH), ACT_DT),
            jax.ShapeDtypeStruct((bsz, nc, SSD_STATE, SSD_INNER), ACT_DT),
            jax.ShapeDtypeStruct((bsz, lp, LANES), F32),
        ],
        scratch_shapes=[pltpu.VMEM((nb, SSD_STATE, SSD_INNER), F32),
                        pltpu.VMEM((nb, d // LANES, Q + 2 * HALO, LANES), F32),
                        pltpu.VMEM((nb, Q, w), F32), pltpu.VMEM((nb, Q, w), F32), pltpu.VMEM((nb, 1, w), F32)],
        compiler_params=_cparams("parallel", "arbitrary"),
        name="fwd_sweep",
    )(h, h, h, lw["w_fwd"], lw["ssd_conv_w"], lw["ssd_conv_b"], lw["dt_bias"], lw["a_log"],
      lw["lru_conv_w"], lw["lru_conv_b"], lw["wg"][0], lw["brg"][0], lw["big"][0], lw["lam"][0])


def _rev_kernel(h_ref, pk_ref, sb_ref, aux_ref,
                wr_ref, dskip_ref, ng_ref, wso_ref, wg_ref, brg_ref, big_ref, lam_ref, wlo_ref,
                wo_ref, g_ref, b_ref, o_ref,
                sb_scr, slab_scr, p_scr, hrow_scr, h_scr):
    @pl.when(pl.program_id(1) == 0)
    def _():
        sb_scr[...] = jnp.zeros_like(sb_scr)
        h_scr[...] = jnp.zeros_like(h_scr)

    for i in range(SEQ_PER_STEP):
        _rev_one(h_ref.at[i], pk_ref.at[i], sb_ref.at[i], aux_ref.at[i],
                 wr_ref, dskip_ref, ng_ref, wso_ref, wg_ref, brg_ref, big_ref, lam_ref, wlo_ref,
                 wo_ref, g_ref, b_ref, o_ref.at[i],
                 sb_scr.at[i], slab_scr.at[i], p_scr.at[i], hrow_scr.at[i], h_scr.at[i])


def _rev_one(h_ref, pk_ref, sb_ref, aux_ref,
             wr_ref, dskip_ref, ng_ref, wso_ref, wg_ref, brg_ref, big_ref, lam_ref, wlo_ref,
             wo_ref, g_ref, b_ref, o_ref,
             sb_scr, slab_scr, p_scr, hrow_scr, h_scr):
    k = pl.program_id(1)
    nc = pl.num_programs(1)
    _to_slabs(h_ref[...], slab_scr)
    h = _interleaved_rows(slab_scr, 0, LRU_SEG)
    pr = _dot(h, wr_ref[...])
    z = pr[:, :SSD_INNER]
    lg = pr[:, SSD_INNER:SSD_INNER + LRU_WIDTH]
    mg = pr[:, SSD_INNER + LRU_WIDTH:]

    y_off = _ssd_reverse(pk_ref[:, PK_CM:PK_YP].astype(F32), aux_ref[...], sb_ref[...].astype(F32), sb_scr)
    y = pk_ref[:, PK_YP:PK_XC].astype(F32) + y_off + pk_ref[:, PK_XS:PK_CM].astype(F32) * dskip_ref[...]
    y = y * _silu(z)
    y = y * lax.rsqrt(jnp.mean(y * y, -1, keepdims=True) + RMS_EPS) * ng_ref[...]
    y_s = _dot(y, wso_ref[...])

    a, u = _lru_gates(pk_ref[:, PK_XC:PK_HF].astype(F32), wg_ref, brg_ref[...], big_ref[...], lam_ref[...])
    _lru_scan(a, u, p_scr, hrow_scr, h_scr, reverse=True)
    y_l = _dot((pk_ref[:, PK_HF:].astype(F32) + hrow_scr[...]) * _gelu(lg), wlo_ref[...])

    mix = _sigmoid(mg[:, :D_MODEL]) * y_s + _sigmoid(mg[:, D_MODEL:]) * y_l
    out = _ln(ALPHA * h + _dot(mix, wo_ref[...]), g_ref[...], b_ref[...])
    out = jnp.where(jnp.logical_or(k < nc - 1, _row_time(_row_iota(Q)) >= PAD), out, 0.0)
    _to_slabs(out, slab_scr)
    per_seg = LRU_SEG // SUBLANES
    for i in range(Q // SUBLANES):
        start = (i % per_seg) * SUBLANES * SUBLANES + i // per_seg
        for s in range(slab_scr.shape[0]):
            o_ref[i * SUBLANES:(i + 1) * SUBLANES, s * LANES:(s + 1) * LANES] = (
                slab_scr[s, pl.ds(start, SUBLANES, stride=SUBLANES), :])


def _rev(h, pk, sb, aux, lw):
    bsz, lp, d = h.shape
    nc = lp // Q
    w = LRU_WIDTH
    nb = SEQ_PER_STEP
    seq = lambda n: pl.BlockSpec((nb, Q, n), lambda i, c: (i, nc - 1 - c, 0))
    return pl.pallas_call(
        _rev_kernel,
        grid=(bsz // nb, nc),
        in_specs=[
            seq(d), seq(PK_WIDTH),
            pl.BlockSpec((nb, None, SSD_STATE, SSD_INNER), lambda i, c: (i, nc - 1 - c, 0, 0)),
            seq(LANES),
            _const_spec((d, SSD_INNER + w + 2 * d)),
            _const_spec((1, SSD_INNER)), _const_spec((1, SSD_INNER)), _const_spec((SSD_INNER, d)),
            _const_spec((w // 256, 256, 512)), _const_spec((1, w)), _const_spec((1, w)), _const_spec((1, w)),
            _const_spec((w, d)), _const_spec((d, d)), _const_spec((1, d)), _const_spec((1, d)),
        ],
        out_specs=seq(d),
        out_shape=jax.ShapeDtypeStruct((bsz, lp, d), F32),
        scratch_shapes=[pltpu.VMEM((nb, SSD_STATE, SSD_INNER), F32), pltpu.VMEM((nb, d // LANES, Q, LANES), F32),
                        pltpu.VMEM((nb, Q, w), F32), pltpu.VMEM((nb, Q, w), F32), pltpu.VMEM((nb, 1, w), F32)],
        compiler_params=_cparams("parallel", "arbitrary"),
        name="rev_sweep",
    )(h, pk, sb, aux, lw["w_rev"], lw["dskip"], lw["norm_g"], lw["w_ssd_out"],
      lw["wg"][1], lw["brg"][1], lw["big"][1], lw["lam"][1], lw["w_lru_out"], lw["w_o"],
      lw["ln1_g"], lw["ln1_b"])


def _ffn_kernel(hp_ref, hc_ref, hn_ref, wu_ref, cw_ref, cb_ref, wd_ref, g_ref, b_ref, o_ref,
                ext_scr, out_scr):
    c = pl.program_id(1)
    nc = pl.num_programs(1)
    tf = hc_ref.shape[0]
    nseg = tf // SEG
    slabs = D_MODEL // LANES
    keep_next = jnp.where(c < nc - 1, 1.0, 0.0)
    for s in range(slabs):
        ls = slice(s * LANES, (s + 1) * LANES)
        ext_scr[s, 0:HALO, :] = hp_ref[:, ls]
        ext_scr[s, HALO:HALO + tf, :] = hc_ref[:, ls]
        ext_scr[s, HALO + tf:, :] = hn_ref[:, ls] * keep_next

    def group(g):
        halves = [jnp.concatenate([ext_scr[s, pl.ds(HALO + g + k * SUBLANES * nseg, SUBLANES, stride=nseg), :]
                                   for s in range(slabs)], axis=1) for k in range(SEG // SUBLANES)]
        return jnp.concatenate(halves, axis=0)
    xperm = jnp.concatenate([group(g) for g in range(-1, nseg + 1)], axis=0)
    xb = xperm.astype(BF16)

    def conv3(up, col):
        cw = cw_ref[:, pl.ds(col, FFN_COLS)].astype(BF16)
        acc = cb_ref[:, pl.ds(col, FFN_COLS)].astype(BF16) + up[0:tf] * cw[0:1, :]
        acc = acc + up[SEG:SEG + tf] * cw[1:2, :]
        return acc + up[2 * SEG:2 * SEG + tf] * cw[2:3, :]

    f = None
    for j in range(D_FF // FFN_COLS):
        cg, cv = j * FFN_COLS, D_FF + j * FFN_COLS
        ug = conv3(jnp.dot(xb, wu_ref[:, pl.ds(cg, FFN_COLS)], preferred_element_type=F32).astype(BF16), cg)
        uv = conv3(jnp.dot(xb, wu_ref[:, pl.ds(cv, FFN_COLS)], preferred_element_type=F32).astype(BF16), cv)
        part = jnp.dot(_gelu(ug) * uv, wd_ref[pl.ds(cg, FFN_COLS), :], preferred_element_type=F32)
        f = part if f is None else f + part
    y = _ln(ALPHA * xperm[SEG:SEG + tf] + f, g_ref[...], b_ref[...])
    p = _row_iota(tf)
    t = c * tf + (p & (SEG - 1)) * nseg + (p >> SEG_SHIFT)
    y = jnp.where(t >= PAD, y, 0.0)
    for j in range(nseg):
        for k in range(SEG // SUBLANES):
            r0 = j * SEG + k * SUBLANES
            for s in range(slabs):
                out_scr[s, pl.ds(j + k * SUBLANES * nseg, SUBLANES, stride=nseg), :] = (
                    y[r0:r0 + SUBLANES, s * LANES:(s + 1) * LANES])
    for s in range(slabs):
        o_ref[:, s * LANES:(s + 1) * LANES] = out_scr[s]


def _ffn_tile(lp):
    for parts in (4, 8, 2, 1):
        if lp % (parts * SEG) == 0 and lp // parts <= 640:
            return lp // parts
    return Q


def _ffn(h, lw):
    bsz, lp, d = h.shape
    tf = _ffn_tile(lp)
    nh = lp // HALO
    per = tf // HALO
    seq = pl.BlockSpec((None, tf, d), lambda i, c: (i, c, 0))
    return pl.pallas_call(
        _ffn_kernel,
        grid=(bsz, lp // tf),
        in_specs=[
            pl.BlockSpec((None, HALO, d), lambda i, c: (i, jnp.maximum(c * per - 1, 0), 0)),
            seq,
            pl.BlockSpec((None, HALO, d), lambda i, c: (i, jnp.minimum((c + 1) * per, nh - 1), 0)),
            _const_spec((d, 2 * D_FF)), _const_spec((3, 2 * D_FF)), _const_spec((1, 2 * D_FF)),
            _const_spec((D_FF, d)), _const_spec((1, d)), _const_spec((1, d)),
        ],
        out_specs=seq,
        out_shape=jax.ShapeDtypeStruct((bsz, lp, d), F32),
        scratch_shapes=[pltpu.VMEM((d // LANES, tf + 2 * HALO, LANES), F32),
                        pltpu.VMEM((d // LANES, tf, LANES), F32)],
        compiler_params=_cparams("parallel", "parallel"),
        name="ffn_ln",
    )(h, h, h, lw["w_up"], lw["ffn_conv_w"], lw["ffn_conv_b"], lw["w_down"], lw["ln2_g"], lw["ln2_b"])


def _pad_lanes(v):
    return jnp.pad(v.reshape(1, -1).astype(F32), ((0, 0), (0, LANES - v.size)))


def _gate_tiles(w_rg, w_ig):
    def tiles(w):
        w4 = w.reshape(LRU_HEADS // 4, 4, LRU_BLOCK, LRU_BLOCK)
        eye = jnp.eye(4, dtype=w.dtype)
        return jnp.einsum("thij,hk->thikj", w4, eye).reshape(LRU_HEADS // 4, 256, 256)
    return jnp.concatenate([tiles(w_rg), tiles(w_ig)], axis=-1).astype(BF16)


def _prep_layer(p, li):
    w_in = p["w_in"][li]
    row = lambda v: v.reshape(1, -1).astype(F32)
    w_dt = jnp.pad(w_in[:, S2:S3], ((0, 0), (0, LANES - (S3 - S2))))
    return dict(
        w_fwd=jnp.concatenate([w_in[:, S1:S2], w_in[:, S4:S5], w_dt], axis=1).astype(BF16),
        w_rev=jnp.concatenate([w_in[:, :S1], w_in[:, S3:S4], w_in[:, S5:]], axis=1).astype(BF16),
        ssd_conv_w=p["ssd_conv_w"][li], ssd_conv_b=row(p["ssd_conv_b"][li]),
        dt_bias=_pad_lanes(p["ssd_dt_bias"][li]), a_log=_pad_lanes(p["ssd_a_log"][li]),
        dskip=row(jnp.repeat(p["ssd_d"][li], SSD_HEAD_DIM)), norm_g=row(p["ssd_norm_g"][li]),
        w_ssd_out=p["w_ssd_out"][li].astype(BF16),
        lru_conv_w=p["lru_conv_w"][li], lru_conv_b=row(p["lru_conv_b"][li]),
        wg=[_gate_tiles(p["lru_w_rg"][li, d], p["lru_w_ig"][li, d]) for d in range(2)],
        brg=[row(p["lru_b_rg"][li, d]) for d in range(2)],
        big=[row(p["lru_b_ig"][li, d]) for d in range(2)],
        lam=[row(p["lru_lambda"][li, d]) for d in range(2)],
        w_lru_out=p["w_lru_out"][li].astype(BF16), w_o=p["w_o"][li].astype(BF16),
        ln1_g=row(p["ln1_g"][li]), ln1_b=row(p["ln1_b"][li]),
        w_up=p["w_up"][li].astype(BF16), ffn_conv_w=p["ffn_conv_w"][li],
        ffn_conv_b=row(p["ffn_conv_b"][li]), w_down=p["w_down"][li].astype(BF16),
        ln2_g=row(p["ln2_g"][li]), ln2_b=row(p["ln2_b"][li]),
    )


def _layer(h, lw):
    pk, sb, aux = _fwd(h, lw)
    h = _rev(h, pk, sb, aux, lw)
    return _ffn(h, lw)


def _run(x, meta_pad, ln_g, ln_b, layers):
    h = _embed(x, meta_pad, ln_g, ln_b)
    for lw in layers:
        h = _layer(h, lw)
    return h[:, Q:]


def kernel(x_prompt, x_sample, meta, ln_in_g, ln_in_b, w_in, ssd_conv_w, ssd_conv_b, ssd_dt_bias, ssd_a_log, ssd_d, ssd_norm_g, w_ssd_out, lru_conv_w, lru_conv_b, lru_w_rg, lru_b_rg, lru_w_ig, lru_b_ig, lru_lambda, w_lru_out, w_o, ln1_g, ln1_b, w_up, ffn_conv_w, ffn_conv_b, w_down, ln2_g, ln2_b):
    p = dict(w_in=w_in, ssd_conv_w=ssd_conv_w, ssd_conv_b=ssd_conv_b, ssd_dt_bias=ssd_dt_bias,
             ssd_a_log=ssd_a_log, ssd_d=ssd_d, ssd_norm_g=ssd_norm_g, w_ssd_out=w_ssd_out,
             lru_conv_w=lru_conv_w, lru_conv_b=lru_conv_b, lru_w_rg=lru_w_rg, lru_b_rg=lru_b_rg,
             lru_w_ig=lru_w_ig, lru_b_ig=lru_b_ig, lru_lambda=lru_lambda, w_lru_out=w_lru_out,
             w_o=w_o, ln1_g=ln1_g, ln1_b=ln1_b, w_up=w_up, ffn_conv_w=ffn_conv_w,
             ffn_conv_b=ffn_conv_b, w_down=w_down, ln2_g=ln2_g, ln2_b=ln2_b)
    layers = [_prep_layer(p, li) for li in range(w_in.shape[0])]
    meta_pad = jnp.pad(meta.astype(F32), ((PAD, 0), (0, 0)))
    g = ln_in_g.reshape(1, -1)
    b = ln_in_b.reshape(1, -1)
    return (_run(x_prompt, meta_pad, g, b, layers), _run(x_sample, meta_pad, g, b, layers))
```

```python
import math

import jax
import jax.numpy as jnp
from jax import lax
from jax.experimental import pallas as pl
from jax.experimental.pallas import tpu as pltpu

F32 = jnp.float32
BF16 = jnp.bfloat16

D_MODEL = 1024
DEPTH = 4
N_META = 16
SSD_HEADS = 16
SSD_HEAD_DIM = 64
SSD_INNER = SSD_HEADS * SSD_HEAD_DIM
SSD_GROUPS = 4
SSD_STATE = 128
BC_DIM = SSD_GROUPS * SSD_STATE
XBC_DIM = SSD_INNER + 2 * BC_DIM
LRU_WIDTH = 1024
LRU_HEADS = 16
LRU_BLOCK = LRU_WIDTH // LRU_HEADS
LRU_C = 8.0
D_FF = 3 * D_MODEL
S1 = SSD_INNER
S2 = S1 + XBC_DIM
S3 = S2 + 2 * SSD_HEADS
S4 = S3 + LRU_WIDTH
S5 = S4 + LRU_WIDTH
ALPHA = (2 * DEPTH) ** 0.25
LN_EPS = 1e-5
RMS_EPS = 1e-6

Q = 128
PAD = Q - N_META
HALO = 8
LANES = 128
SUBLANES = 8
LRU_SEG = Q // SUBLANES
SEG_SHIFT = 4
SEG = 1 << SEG_SHIFT
FFN_COLS = 512
ACT_DT = BF16
SEQ_PER_STEP = 2
VMEM_LIMIT = 56 * 1024 * 1024


def _cparams(*sem):
    return pltpu.CompilerParams(dimension_semantics=sem, vmem_limit_bytes=VMEM_LIMIT)


def _const_spec(shape):
    zeros = (0,) * len(shape)
    return pl.BlockSpec(shape, lambda i, c: zeros, pipeline_mode=pl.Buffered(1))


def _dot(a, b):
    return jnp.dot(a.astype(BF16), b.astype(BF16), preferred_element_type=F32)


def _dot_nt(a, b):
    return lax.dot_general(a.astype(BF16), b.astype(BF16), (((1,), (1,)), ((), ())),
                           preferred_element_type=F32)


def _split_dot(v, e, terms, e_left=False):
    acc = None
    r = v
    for _ in range(terms):
        hi = r.astype(BF16)
        part = (jnp.dot(e, hi, preferred_element_type=F32) if e_left
                else jnp.dot(hi, e, preferred_element_type=F32))
        acc = part if acc is None else acc + part
        r = r - hi.astype(F32)
    return acc


def _ln(x, g, b):
    mu = jnp.mean(x, -1, keepdims=True)
    xc = x - mu
    var = jnp.mean(xc * xc, -1, keepdims=True)
    return xc * lax.rsqrt(var + LN_EPS) * g + b


def _sigmoid(x):
    return 1.0 / (1.0 + jnp.exp(-x))


def _silu(x):
    return x * _sigmoid(x)


def _gelu(x):
    c = math.sqrt(2.0 / math.pi)
    return x * (0.5 * (1.0 + jnp.tanh(c * (x + 0.044715 * (x * x * x)))))


def _log1p(e):
    u = 1.0 + e
    return jnp.where(u == 1.0, e, jnp.log(u) * (e / (u - 1.0)))


def _softplus(x):
    return jnp.maximum(x, 0.0) + _log1p(jnp.exp(-jnp.abs(x)))


def _row_iota(n):
    return lax.broadcasted_iota(jnp.int32, (n, 1), 0)


def _row_time(p):
    return (p & (SUBLANES - 1)) * LRU_SEG + (p >> 3)


def _embed_kernel(x_ref, meta_ref, g_ref, b_ref, o_ref):
    c = pl.program_id(1)
    src = jnp.where(c == 0, meta_ref[...], x_ref[...])
    y = _ln(src, g_ref[...], b_ref[...])
    keep = jnp.logical_or(c > 0, _row_iota(Q) >= PAD)
    o_ref[...] = jnp.where(keep, y, 0.0)


def _embed(x, meta_pad, g, b):
    bsz, t, d = x.shape
    nc = t // Q + 1
    return pl.pallas_call(
        _embed_kernel,
        grid=(bsz, nc),
        in_specs=[
            pl.BlockSpec((None, Q, d), lambda i, c: (i, jnp.maximum(c - 1, 0), 0)),
            pl.BlockSpec((Q, d), lambda i, c: (0, 0)),
            pl.BlockSpec((1, d), lambda i, c: (0, 0)),
            pl.BlockSpec((1, d), lambda i, c: (0, 0)),
        ],
        out_specs=pl.BlockSpec((None, Q, d), lambda i, c: (i, c, 0)),
        out_shape=jax.ShapeDtypeStruct((bsz, nc * Q, d), F32),
        compiler_params=_cparams("parallel", "parallel"),
        name="embed_ln",
    )(x, meta_pad, g, b)


def _head_onehot(offset):
    r = lax.broadcasted_iota(jnp.int32, (LANES, SSD_INNER), 0)
    c = lax.broadcasted_iota(jnp.int32, (LANES, SSD_INNER), 1)
    return jnp.where(r - offset == c // SSD_HEAD_DIM, 1.0, 0.0).astype(BF16)


def _pair_rhs(x_pair):
    lane = lax.broadcasted_iota(jnp.int32, x_pair.shape, 1)
    lo = jnp.where(lane < SSD_HEAD_DIM, x_pair, 0.0)
    hi = jnp.where(lane >= SSD_HEAD_DIM, x_pair, 0.0)
    return jnp.concatenate([lo, hi], axis=0)


def _ssd_decay(dt_raw, dtb, alog, first):
    lane = lax.broadcasted_iota(jnp.int32, (1, LANES), 1)
    dt = _softplus(dt_raw + dtb)
    dt = jnp.where(jnp.logical_or(jnp.logical_not(first), _row_time(_row_iota(Q)) >= PAD), dt, 0.0)
    a = dt * (-jnp.exp(alog))
    ti = _row_time(lax.broadcasted_iota(jnp.int32, (Q, Q), 0))
    si = _row_time(lax.broadcasted_iota(jnp.int32, (Q, Q), 1))
    lower = jnp.where(si <= ti, 1.0, 0.0).astype(BF16)
    upper = jnp.where(si >= ti, 1.0, 0.0).astype(BF16)
    cs = jnp.where(lane < SSD_HEADS, _split_dot(a, lower, 3, True), _split_dot(a, upper, 3, True))
    edge = jnp.where(lane < SSD_HEADS, cs[Q - 1:Q, :], cs[0:1, :])
    wsrc = dt * jnp.exp(edge - cs)
    ecs = jnp.exp(cs)
    return dict(
        cs=cs, ecs=ecs, cs_t=cs.T, dt_t=dt.T, lt=si < ti, eq=si == ti,
        wf=_split_dot(wsrc, _head_onehot(0), 2), wb=_split_dot(wsrc, _head_onehot(SSD_HEADS), 2),
        dec=_split_dot(jnp.broadcast_to(ecs[Q - 1:Q, :], (8, LANES)), _head_onehot(0), 2)[0:1, :])


def _ssd_forward(xs, bm, cm, dk, sf_scr):
    cs, ecs, cs_t, dt_t, lt, eq = dk["cs"], dk["ecs"], dk["cs_t"], dk["dt_t"], dk["lt"], dk["eq"]
    sf = sf_scr[...]
    sf_b = sf.astype(BF16)
    xs_b = xs.astype(BF16)
    ys = []
    for j in range(SSD_HEADS // 2):
        g = (2 * j) // (SSD_HEADS // SSD_GROUPS)
        c_g = cm[:, g * SSD_STATE:(g + 1) * SSD_STATE]
        b_g = bm[:, g * SSD_STATE:(g + 1) * SSD_STATE]
        gmat = _dot_nt(c_g, b_g)
        lhs = []
        for h in (2 * j, 2 * j + 1):
            hb = SSD_HEADS + h
            e = jnp.where(lt, cs[:, h:h + 1] - cs_t[h:h + 1, :], cs[:, hb:hb + 1] - cs_t[hb:hb + 1, :])
            dsel = (jnp.where(lt, dt_t[h:h + 1, :], dt_t[hb:hb + 1, :])
                    + jnp.where(eq, dt_t[h:h + 1, :], 0.0))
            lhs.append((gmat * jnp.exp(e) * dsel).astype(BF16))
        for h in (2 * j, 2 * j + 1):
            lhs.append((c_g * ecs[:, h:h + 1]).astype(BF16))
        lhs = jnp.concatenate(lhs, axis=1)
        sl = slice(2 * j * SSD_HEAD_DIM, (2 * j + 2) * SSD_HEAD_DIM)
        rhs = jnp.concatenate([_pair_rhs(xs_b[:, sl]), _pair_rhs(sf_b[:, sl])], axis=0)
        ys.append(jnp.dot(lhs, rhs, preferred_element_type=F32))
    y_part = jnp.concatenate(ys, axis=1)

    xf = (xs * dk["wf"]).astype(BF16)
    xb = (xs * dk["wb"]).astype(BF16)
    st_f = []
    st_b = []
    for g in range(SSD_GROUPS):
        b_gt = bm[:, g * SSD_STATE:(g + 1) * SSD_STATE].T.astype(BF16)
        sl = slice(g * 256, (g + 1) * 256)
        st_f.append(jnp.dot(b_gt, xf[:, sl], preferred_element_type=F32))
        st_b.append(jnp.dot(b_gt, xb[:, sl], preferred_element_type=F32))
    sf_scr[...] = sf * dk["dec"] + jnp.concatenate(st_f, axis=1)
    return y_part, jnp.concatenate(st_b, axis=1)


def _ssd_reverse(cm, ecs, sb_chunk, sb_scr):
    sb = sb_scr[...]
    sb_b = sb.astype(BF16)
    ys = []
    for j in range(SSD_HEADS // 2):
        g = (2 * j) // (SSD_HEADS // SSD_GROUPS)
        c_g = cm[:, g * SSD_STATE:(g + 1) * SSD_STATE]
        lhs = [(c_g * ecs[:, SSD_HEADS + h:SSD_HEADS + h + 1]).astype(BF16) for h in (2 * j, 2 * j + 1)]
        sl = slice(2 * j * SSD_HEAD_DIM, (2 * j + 2) * SSD_HEAD_DIM)
        ys.append(jnp.dot(jnp.concatenate(lhs, axis=1), _pair_rhs(sb_b[:, sl]), preferred_element_type=F32))
    dec = _split_dot(jnp.broadcast_to(ecs[0:1, :], (8, LANES)), _head_onehot(SSD_HEADS), 2)[0:1, :]
    sb_scr[...] = sb * dec + sb_chunk
    return jnp.concatenate(ys, axis=1)


LRU_TILE = 4 * LRU_BLOCK
LRU_TILES = LRU_WIDTH // LRU_TILE


def _lru_gates(xc, wg, brg, big, lam):
    pre = jnp.dot(xc.astype(BF16), wg, preferred_element_type=F32)
    r = _sigmoid(pre[:, :LRU_TILE] + brg)
    i = _sigmoid(pre[:, LRU_TILE:] + big)
    log_a = (-LRU_C) * r * _softplus(-lam)
    a = jnp.exp(log_a)
    u = jnp.sqrt(-jnp.tanh(log_a) * (a * a + 1.0)) * (i * xc)
    return a, u


def _lru_scan(a, u, p_scr, o_scr, h_scr, cols, reverse):
    order = range(LRU_SEG - 1, -1, -1) if reverse else range(LRU_SEG)
    h = jnp.zeros((SUBLANES, a.shape[1]), F32)
    p = jnp.ones((SUBLANES, a.shape[1]), F32)
    for j in order:
        rows = slice(j * SUBLANES, (j + 1) * SUBLANES)
        h = a[rows] * h + u[rows]
        p = a[rows] * p
        o_scr[rows, cols] = h
        p_scr[rows, cols] = p
    carry = h_scr[:, cols]
    enter = [None] * SUBLANES
    for s in (range(SUBLANES - 1, -1, -1) if reverse else range(SUBLANES)):
        enter[s] = carry
        carry = h[s:s + 1, :] + p[s:s + 1, :] * carry
    h_scr[:, cols] = carry
    enter = jnp.concatenate(enter, axis=0)
    for j in order:
        rows = slice(j * SUBLANES, (j + 1) * SUBLANES)
        o_scr[rows, cols] = o_scr[rows, cols] + p_scr[rows, cols] * enter


def _lru_tile(j, xc, live, wg_ref, brg_ref, big_ref, lam_ref, p_scr, o_scr, h_scr, reverse):
    cols = slice(j * LRU_TILE, (j + 1) * LRU_TILE)
    a, u = _lru_gates(xc, wg_ref[j], brg_ref[:, cols], big_ref[:, cols], lam_ref[:, cols])
    if live is not None:
        u = jnp.where(live, u, 0.0)
    _lru_scan(a, u, p_scr, o_scr, h_scr, cols, reverse)


def _interleaved_rows(slab_scr, first, count):
    groups = [jnp.concatenate([slab_scr[s, pl.ds(first + g, SUBLANES, stride=LRU_SEG), :]
                               for s in range(slab_scr.shape[0])], axis=1) for g in range(count)]
    return jnp.concatenate(groups, axis=0)


def _to_slabs(x, slab_scr):
    for s in range(slab_scr.shape[0]):
        slab_scr[s] = x[:, s * LANES:(s + 1) * LANES]


def _dwconv4(pe, cw, cb):
    acc = cb
    for k in range(4):
        acc = acc + pe[k * SUBLANES:k * SUBLANES + Q] * cw[k:k + 1, :]
    return acc


PK_XS, PK_CM, PK_YP, PK_XC, PK_HF = 0, SSD_INNER, SSD_INNER + BC_DIM, 2 * SSD_INNER + BC_DIM, 2 * SSD_INNER + BC_DIM + LRU_WIDTH
PK_WIDTH = PK_HF + LRU_WIDTH


def _fwd_kernel(hp_ref, hc_ref, hn_ref, wf_ref, scw_ref, scb_ref, dtb_ref, alog_ref,
                lcw_ref, lcb_ref, wg_ref, brg_ref, big_ref, lam_ref,
                pk_o, sb_o, aux_o,
                sf_scr, ext_scr, p_scr, hrow_scr, h_scr):
    c = pl.program_id(1)
    nc = pl.num_programs(1)
    first = c == 0

    @pl.when(first)
    def _():
        sf_scr[...] = jnp.zeros_like(sf_scr)
        h_scr[...] = jnp.zeros_like(h_scr)

    for i in range(SEQ_PER_STEP):
        pk = pk_o.at[i]
        nxt = hn_ref[i] * jnp.where(c < nc - 1, 1.0, 0.0)
        _to_slabs(jnp.concatenate([hp_ref[i], hc_ref[i], nxt], axis=0), ext_scr.at[i])
        xl = _interleaved_rows(ext_scr.at[i], HALO - 2, LRU_SEG + 3).astype(BF16)
        pe_dt = jnp.dot(xl[2 * SUBLANES:2 * SUBLANES + Q], wf_ref[:, XBC_DIM + LRU_WIDTH:],
                        preferred_element_type=F32)
        pe_l = jnp.dot(xl, wf_ref[:, XBC_DIM:XBC_DIM + LRU_WIDTH], preferred_element_type=F32)

        dk = _ssd_decay(pe_dt, dtb_ref[...], alog_ref[...], first)
        aux_o[i] = dk["ecs"]

        live = jnp.logical_or(jnp.logical_not(first), _row_time(_row_iota(Q)) >= PAD)
        pe_s = []
        for j in range(LRU_TILES):
            cols = slice(j * LRU_TILE, (j + 1) * LRU_TILE)
            pe_s.append(jnp.dot(xl, wf_ref[:, 2 * j * LRU_TILE:2 * (j + 1) * LRU_TILE],
                                preferred_element_type=F32))
            xc = _dwconv4(pe_l[:, cols], lcw_ref[:, cols], lcb_ref[:, cols])
            pk[:, PK_XC + j * LRU_TILE:PK_XC + (j + 1) * LRU_TILE] = xc.astype(pk.dtype)
            _lru_tile(j, xc, live, wg_ref, brg_ref, big_ref, lam_ref, p_scr.at[i], hrow_scr.at[i],
                      h_scr.at[i], reverse=False)
        pk[:, PK_HF:] = hrow_scr[i].astype(pk.dtype)
        pe_x = jnp.concatenate(pe_s[:LRU_TILES // 2], axis=1)
        pe_bc = jnp.concatenate(pe_s[LRU_TILES // 2:], axis=1)

        xs = _silu(_dwconv4(pe_x, scw_ref[:, :SSD_INNER], scb_ref[:, :SSD_INNER]))
        bc = _silu(_dwconv4(pe_bc, scw_ref[:, SSD_INNER:], scb_ref[:, SSD_INNER:]))
        pk[:, PK_XS:PK_CM] = xs.astype(pk.dtype)
        pk[:, PK_CM:PK_YP] = bc[:, BC_DIM:].astype(pk.dtype)
        y_part, st_b = _ssd_forward(xs, bc[:, :BC_DIM], bc[:, BC_DIM:], dk, sf_scr.at[i])
        pk[:, PK_YP:PK_XC] = y_part.astype(pk.dtype)
        sb_o[i] = st_b.astype(sb_o.dtype)


def _fwd(h, lw):
    bsz, lp, d = h.shape
    nc = lp // Q
    nh = lp // HALO
    per = Q // HALO
    w = LRU_WIDTH
    nb = SEQ_PER_STEP
    seq = lambda n: pl.BlockSpec((nb, Q, n), lambda i, c: (i, c, 0))
    return pl.pallas_call(
        _fwd_kernel,
        grid=(bsz // nb, nc),
        in_specs=[
            pl.BlockSpec((nb, HALO, d), lambda i, c: (i, jnp.maximum(c * per - 1, 0), 0)),
            seq(d),
            pl.BlockSpec((nb, HALO, d), lambda i, c: (i, jnp.minimum((c + 1) * per, nh - 1), 0)),
            _const_spec((d, XBC_DIM + w + LANES)),
            _const_spec((4, XBC_DIM)), _const_spec((1, XBC_DIM)), _const_spec((1, LANES)), _const_spec((1, LANES)),
            _const_spec((4, w)), _const_spec((1, w)), _const_spec((w // 256, 256, 512)),
            _const_spec((1, w)), _const_spec((1, w)), _const_spec((1, w)),
        ],
        out_specs=[
            seq(PK_WIDTH),
            pl.BlockSpec((nb, None, SSD_STATE, SSD_INNER), lambda i, c: (i, c, 0, 0)),
            seq(LANES),
        ],
        out_shape=[
            jax.ShapeDtypeStruct((bsz, lp, PK_WIDTH), ACT_DT),
            jax.ShapeDtypeStruct((bsz, nc, SSD_STATE, SSD_INNER), ACT_DT),
            jax.ShapeDtypeStruct((bsz, lp, LANES), F32),
        ],
        scratch_shapes=[pltpu.VMEM((nb, SSD_STATE, SSD_INNER), F32),
                        pltpu.VMEM((nb, d // LANES, Q + 2 * HALO, LANES), F32),
                        pltpu.VMEM((nb, Q, w), F32), pltpu.VMEM((nb, Q, w), F32), pltpu.VMEM((nb, 1, w), F32)],
        compiler_params=_cparams("parallel", "arbitrary"),
        name="fwd_sweep",
    )(h, h, h, lw["w_fwd"], lw["ssd_conv_w"], lw["ssd_conv_b"], lw["dt_bias"], lw["a_log"],
      lw["lru_conv_w"], lw["lru_conv_b"], lw["wg"][0], lw["brg"][0], lw["big"][0], lw["lam"][0])


def _rev_kernel(h_ref, pk_ref, sb_ref, aux_ref,
                wr_ref, dskip_ref, ng_ref, wso_ref, wg_ref, brg_ref, big_ref, lam_ref, wlo_ref,
                wo_ref, g_ref, b_ref, o_ref,
                sb_scr, slab_scr, p_scr, hrow_scr, h_scr):
    @pl.when(pl.program_id(1) == 0)
    def _():
        sb_scr[...] = jnp.zeros_like(sb_scr)
        h_scr[...] = jnp.zeros_like(h_scr)

    for i in range(SEQ_PER_STEP):
        _rev_one(h_ref.at[i], pk_ref.at[i], sb_ref.at[i], aux_ref.at[i],
                 wr_ref, dskip_ref, ng_ref, wso_ref, wg_ref, brg_ref, big_ref, lam_ref, wlo_ref,
                 wo_ref, g_ref, b_ref, o_ref.at[i],
                 sb_scr.at[i], slab_scr.at[i], p_scr.at[i], hrow_scr.at[i], h_scr.at[i])


def _rev_one(h_ref, pk_ref, sb_ref, aux_ref,
             wr_ref, dskip_ref, ng_ref, wso_ref, wg_ref, brg_ref, big_ref, lam_ref, wlo_ref,
             wo_ref, g_ref, b_ref, o_ref,
             sb_scr, slab_scr, p_scr, hrow_scr, h_scr):
    k = pl.program_id(1)
    nc = pl.num_programs(1)
    _to_slabs(h_ref[...], slab_scr)
    h = _interleaved_rows(slab_scr, 0, LRU_SEG)
    hb = h.astype(BF16)
    zl = []
    for j in range(LRU_TILES):
        zl.append(jnp.dot(hb, wr_ref[:, 2 * j * LRU_TILE:2 * (j + 1) * LRU_TILE], preferred_element_type=F32))
        xc = pk_ref[:, PK_XC + j * LRU_TILE:PK_XC + (j + 1) * LRU_TILE].astype(F32)
        _lru_tile(j, xc, None, wg_ref, brg_ref, big_ref, lam_ref, p_scr, hrow_scr, h_scr, reverse=True)
    z = jnp.concatenate(zl[:LRU_TILES // 2], axis=1)
    lg = jnp.concatenate(zl[LRU_TILES // 2:], axis=1)

    y_off = _ssd_reverse(pk_ref[:, PK_CM:PK_YP].astype(F32), aux_ref[...], sb_ref[...].astype(F32), sb_scr)
    y = pk_ref[:, PK_YP:PK_XC].astype(F32) + y_off + pk_ref[:, PK_XS:PK_CM].astype(F32) * dskip_ref[...]
    y = y * _silu(z)
    y = y * lax.rsqrt(jnp.mean(y * y, -1, keepdims=True) + RMS_EPS) * ng_ref[...]
    y_s = _dot(y, wso_ref[...])

    mg = jnp.dot(hb, wr_ref[:, SSD_INNER + LRU_WIDTH:], preferred_element_type=F32)

    y_l = _dot((pk_ref[:, PK_HF:].astype(F32) + hrow_scr[...]) * _gelu(lg), wlo_ref[...])

    mix = _sigmoid(mg[:, :D_MODEL]) * y_s + _sigmoid(mg[:, D_MODEL:]) * y_l
    out = _ln(ALPHA * h + _dot(mix, wo_ref[...]), g_ref[...], b_ref[...])
    out = jnp.where(jnp.logical_or(k < nc - 1, _row_time(_row_iota(Q)) >= PAD), out, 0.0)
    _to_slabs(out, slab_scr)
    per_seg = LRU_SEG // SUBLANES
    for i in range(Q // SUBLANES):
        start = (i % per_seg) * SUBLANES * SUBLANES + i // per_seg
        for s in range(slab_scr.shape[0]):
            o_ref[i * SUBLANES:(i + 1) * SUBLANES, s * LANES:(s + 1) * LANES] = (
                slab_scr[s, pl.ds(start, SUBLANES, stride=SUBLANES), :])


def _rev(h, pk, sb, aux, lw):
    bsz, lp, d = h.shape
    nc = lp // Q
    w = LRU_WIDTH
    nb = SEQ_PER_STEP
    seq = lambda n: pl.BlockSpec((nb, Q, n), lambda i, c: (i, nc - 1 - c, 0))
    return pl.pallas_call(
        _rev_kernel,
        grid=(bsz // nb, nc),
        in_specs=[
            seq(d), seq(PK_WIDTH),
            pl.BlockSpec((nb, None, SSD_STATE, SSD_INNER), lambda i, c: (i, nc - 1 - c, 0, 0)),
            seq(LANES),
            _const_spec((d, SSD_INNER + w + 2 * d)),
            _const_spec((1, SSD_INNER)), _const_spec((1, SSD_INNER)), _const_spec((SSD_INNER, d)),
            _const_spec((w // 256, 256, 512)), _const_spec((1, w)), _const_spec((1, w)), _const_spec((1, w)),
            _const_spec((w, d)), _const_spec((d, d)), _const_spec((1, d)), _const_spec((1, d)),
        ],
        out_specs=seq(d),
        out_shape=jax.ShapeDtypeStruct((bsz, lp, d), F32),
        scratch_shapes=[pltpu.VMEM((nb, SSD_STATE, SSD_INNER), F32), pltpu.VMEM((nb, d // LANES, Q, LANES), F32),
                        pltpu.VMEM((nb, Q, w), F32), pltpu.VMEM((nb, Q, w), F32), pltpu.VMEM((nb, 1, w), F32)],
        compiler_params=_cparams("parallel", "arbitrary"),
        name="rev_sweep",
    )(h, pk, sb, aux, lw["w_rev"], lw["dskip"], lw["norm_g"], lw["w_ssd_out"],
      lw["wg"][1], lw["brg"][1], lw["big"][1], lw["lam"][1], lw["w_lru_out"], lw["w_o"],
      lw["ln1_g"], lw["ln1_b"])


def _ffn_kernel(hp_ref, hc_ref, hn_ref, wu_ref, cw_ref, cb_ref, wd_ref, g_ref, b_ref, o_ref,
                ext_scr, out_scr):
    c = pl.program_id(1)
    nc = pl.num_programs(1)
    tf = hc_ref.shape[0]
    nseg = tf // SEG
    slabs = D_MODEL // LANES
    keep_next = jnp.where(c < nc - 1, 1.0, 0.0)
    for s in range(slabs):
        ls = slice(s * LANES, (s + 1) * LANES)
        ext_scr[s, 0:HALO, :] = hp_ref[:, ls]
        ext_scr[s, HALO:HALO + tf, :] = hc_ref[:, ls]
        ext_scr[s, HALO + tf:, :] = hn_ref[:, ls] * keep_next

    def group(g):
        halves = [jnp.concatenate([ext_scr[s, pl.ds(HALO + g + k * SUBLANES * nseg, SUBLANES, stride=nseg), :]
                                   for s in range(slabs)], axis=1) for k in range(SEG // SUBLANES)]
        return jnp.concatenate(halves, axis=0)
    xperm = jnp.concatenate([group(g) for g in range(-1, nseg + 1)], axis=0)
    xb = xperm.astype(BF16)

    def conv3(up, col):
        cw = cw_ref[:, pl.ds(col, FFN_COLS)].astype(BF16)
        acc = cb_ref[:, pl.ds(col, FFN_COLS)].astype(BF16) + up[0:tf] * cw[0:1, :]
        acc = acc + up[SEG:SEG + tf] * cw[1:2, :]
        return acc + up[2 * SEG:2 * SEG + tf] * cw[2:3, :]

    f = None
    for j in range(D_FF // FFN_COLS):
        cg, cv = j * FFN_COLS, D_FF + j * FFN_COLS
        ug = conv3(jnp.dot(xb, wu_ref[:, pl.ds(cg, FFN_COLS)], preferred_element_type=F32).astype(BF16), cg)
        uv = conv3(jnp.dot(xb, wu_ref[:, pl.ds(cv, FFN_COLS)], preferred_element_type=F32).astype(BF16), cv)
        part = jnp.dot(_gelu(ug) * uv, wd_ref[pl.ds(cg, FFN_COLS), :], preferred_element_type=F32)
        f = part if f is None else f + part
    y = _ln(ALPHA * xperm[SEG:SEG + tf] + f, g_ref[...], b_ref[...])
    p = _row_iota(tf)
    t = c * tf + (p & (SEG - 1)) * nseg + (p >> SEG_SHIFT)
    y = jnp.where(t >= PAD, y, 0.0)
    for j in range(nseg):
        for k in range(SEG // SUBLANES):
            r0 = j * SEG + k * SUBLANES
            for s in range(slabs):
                out_scr[s, pl.ds(j + k * SUBLANES * nseg, SUBLANES, stride=nseg), :] = (
                    y[r0:r0 + SUBLANES, s * LANES:(s + 1) * LANES])
    for s in range(slabs):
        o_ref[:, s * LANES:(s + 1) * LANES] = out_scr[s]


def _ffn_tile(lp):
    for parts in (4, 8, 2, 1):
        if lp % (parts * SEG) == 0 and lp // parts <= 640:
            return lp // parts
    return Q


def _ffn(h, lw):
    bsz, lp, d = h.shape
    tf = _ffn_tile(lp)
    nh = lp // HALO
    per = tf // HALO
    seq = pl.BlockSpec((None, tf, d), lambda i, c: (i, c, 0))
    return pl.pallas_call(
        _ffn_kernel,
        grid=(bsz, lp // tf),
        in_specs=[
            pl.BlockSpec((None, HALO, d), lambda i, c: (i, jnp.maximum(c * per - 1, 0), 0)),
            seq,
            pl.BlockSpec((None, HALO, d), lambda i, c: (i, jnp.minimum((c + 1) * per, nh - 1), 0)),
            _const_spec((d, 2 * D_FF)), _const_spec((3, 2 * D_FF)), _const_spec((1, 2 * D_FF)),
            _const_spec((D_FF, d)), _const_spec((1, d)), _const_spec((1, d)),
        ],
        out_specs=seq,
        out_shape=jax.ShapeDtypeStruct((bsz, lp, d), F32),
        scratch_shapes=[pltpu.VMEM((d // LANES, tf + 2 * HALO, LANES), F32),
                        pltpu.VMEM((d // LANES, tf, LANES), F32)],
        compiler_params=_cparams("parallel", "parallel"),
        name="ffn_ln",
    )(h, h, h, lw["w_up"], lw["ffn_conv_w"], lw["ffn_conv_b"], lw["w_down"], lw["ln2_g"], lw["ln2_b"])


def _pad_lanes(v):
    return jnp.pad(v.reshape(1, -1).astype(F32), ((0, 0), (0, LANES - v.size)))


def _gate_tiles(w_rg, w_ig):
    def tiles(w):
        w4 = w.reshape(LRU_HEADS // 4, 4, LRU_BLOCK, LRU_BLOCK)
        eye = jnp.eye(4, dtype=w.dtype)
        return jnp.einsum("thij,hk->thikj", w4, eye).reshape(LRU_HEADS // 4, 256, 256)
    return jnp.concatenate([tiles(w_rg), tiles(w_ig)], axis=-1).astype(BF16)


def _prep_layer(p, li):
    w_in = p["w_in"][li]
    row = lambda v: v.reshape(1, -1).astype(F32)
    w_dt = jnp.pad(w_in[:, S2:S3], ((0, 0), (0, LANES - (S3 - S2))))
    return dict(
        w_fwd=jnp.concatenate([w_in[:, S1:S2], w_in[:, S4:S5], w_dt], axis=1).astype(BF16),
        w_rev=jnp.concatenate([w_in[:, :S1], w_in[:, S3:S4], w_in[:, S5:]], axis=1).astype(BF16),
        ssd_conv_w=p["ssd_conv_w"][li], ssd_conv_b=row(p["ssd_conv_b"][li]),
        dt_bias=_pad_lanes(p["ssd_dt_bias"][li]), a_log=_pad_lanes(p["ssd_a_log"][li]),
        dskip=row(jnp.repeat(p["ssd_d"][li], SSD_HEAD_DIM)), norm_g=row(p["ssd_norm_g"][li]),
        w_ssd_out=p["w_ssd_out"][li].astype(BF16),
        lru_conv_w=p["lru_conv_w"][li], lru_conv_b=row(p["lru_conv_b"][li]),
        wg=[_gate_tiles(p["lru_w_rg"][li, d], p["lru_w_ig"][li, d]) for d in range(2)],
        brg=[row(p["lru_b_rg"][li, d]) for d in range(2)],
        big=[row(p["lru_b_ig"][li, d]) for d in range(2)],
        lam=[row(p["lru_lambda"][li, d]) for d in range(2)],
        w_lru_out=p["w_lru_out"][li].astype(BF16), w_o=p["w_o"][li].astype(BF16),
        ln1_g=row(p["ln1_g"][li]), ln1_b=row(p["ln1_b"][li]),
        w_up=p["w_up"][li].astype(BF16), ffn_conv_w=p["ffn_conv_w"][li],
        ffn_conv_b=row(p["ffn_conv_b"][li]), w_down=p["w_down"][li].astype(BF16),
        ln2_g=row(p["ln2_g"][li]), ln2_b=row(p["ln2_b"][li]),
    )


def _layer(h, lw):
    pk, sb, aux = _fwd(h, lw)
    h = _rev(h, pk, sb, aux, lw)
    return _ffn(h, lw)


def _run(x, meta_pad, ln_g, ln_b, layers):
    h = _embed(x, meta_pad, ln_g, ln_b)
    for lw in layers:
        h = _layer(h, lw)
    return h[:, Q:]


def kernel(x_prompt, x_sample, meta, ln_in_g, ln_in_b, w_in, ssd_conv_w, ssd_conv_b, ssd_dt_bias, ssd_a_log, ssd_d, ssd_norm_g, w_ssd_out, lru_conv_w, lru_conv_b, lru_w_rg, lru_b_rg, lru_w_ig, lru_b_ig, lru_lambda, w_lru_out, w_o, ln1_g, ln1_b, w_up, ffn_conv_w, ffn_conv_b, w_down, ln2_g, ln2_b):
    p = dict(w_in=w_in, ssd_conv_w=ssd_conv_w, ssd_conv_b=ssd_conv_b, ssd_dt_bias=ssd_dt_bias,
             ssd_a_log=ssd_a_log, ssd_d=ssd_d, ssd_norm_g=ssd_norm_g, w_ssd_out=w_ssd_out,
             lru_conv_w=lru_conv_w, lru_conv_b=lru_conv_b, lru_w_rg=lru_w_rg, lru_b_rg=lru_b_rg,
             lru_w_ig=lru_w_ig, lru_b_ig=lru_b_ig, lru_lambda=lru_lambda, w_lru_out=w_lru_out,
             w_o=w_o, ln1_g=ln1_g, ln1_b=ln1_b, w_up=w_up, ffn_conv_w=ffn_conv_w,
             ffn_conv_b=ffn_conv_b, w_down=w_down, ln2_g=ln2_g, ln2_b=ln2_b)
    layers = [_prep_layer(p, li) for li in range(w_in.shape[0])]
    meta_pad = jnp.pad(meta.astype(F32), ((PAD, 0), (0, 0)))
    g = ln_in_g.reshape(1, -1)
    b = ln_in_b.reshape(1, -1)
    return (_run(x_prompt, meta_pad, g, b, layers), _run(x_sample, meta_pad, g, b, layers))
```

```python
import math

import jax
import jax.numpy as jnp
from jax import lax
from jax.experimental import pallas as pl
from jax.experimental.pallas import tpu as pltpu

F32 = jnp.float32
BF16 = jnp.bfloat16

D_MODEL = 1024
DEPTH = 4
N_META = 16
SSD_HEADS = 16
SSD_HEAD_DIM = 64
SSD_INNER = SSD_HEADS * SSD_HEAD_DIM
SSD_GROUPS = 4
SSD_STATE = 128
BC_DIM = SSD_GROUPS * SSD_STATE
XBC_DIM = SSD_INNER + 2 * BC_DIM
LRU_WIDTH = 1024
LRU_HEADS = 16
LRU_BLOCK = LRU_WIDTH // LRU_HEADS
LRU_C = 8.0
D_FF = 3 * D_MODEL
S1 = SSD_INNER
S2 = S1 + XBC_DIM
S3 = S2 + 2 * SSD_HEADS
S4 = S3 + LRU_WIDTH
S5 = S4 + LRU_WIDTH
ALPHA = (2 * DEPTH) ** 0.25
LN_EPS = 1e-5
RMS_EPS = 1e-6

Q = 128
PAD = Q - N_META
HALO = 8
LANES = 128
SUBLANES = 8
LRU_SEG = Q // SUBLANES
FWD_PITCH = LRU_SEG + 2 * HALO + 8
REV_PITCH = LRU_SEG + 8
SEG_SHIFT = 4
SEG = 1 << SEG_SHIFT
FFN_COLS = 512
ACT_DT = BF16
SEQ_PER_STEP = 2
VMEM_LIMIT = 56 * 1024 * 1024


def _cparams(*sem):
    return pltpu.CompilerParams(dimension_semantics=sem, vmem_limit_bytes=VMEM_LIMIT)


def _const_spec(shape):
    zeros = (0,) * len(shape)
    return pl.BlockSpec(shape, lambda i, c: zeros, pipeline_mode=pl.Buffered(1))


def _dot(a, b):
    return jnp.dot(a.astype(BF16), b.astype(BF16), preferred_element_type=F32)


def _dot_nt(a, b):
    return lax.dot_general(a.astype(BF16), b.astype(BF16), (((1,), (1,)), ((), ())),
                           preferred_element_type=F32)


def _split_dot(v, e, terms, e_left=False):
    acc = None
    r = v
    for _ in range(terms):
        hi = r.astype(BF16)
        part = (jnp.dot(e, hi, preferred_element_type=F32) if e_left
                else jnp.dot(hi, e, preferred_element_type=F32))
        acc = part if acc is None else acc + part
        r = r - hi.astype(F32)
    return acc


def _ln(x, g, b):
    mu = jnp.mean(x, -1, keepdims=True)
    xc = x - mu
    var = jnp.mean(xc * xc, -1, keepdims=True)
    return xc * lax.rsqrt(var + LN_EPS) * g + b


def _sigmoid(x):
    return 1.0 / (1.0 + jnp.exp(-x))


def _silu(x):
    return x * _sigmoid(x)


def _gelu(x):
    c = math.sqrt(2.0 / math.pi)
    return x * (0.5 * (1.0 + jnp.tanh(c * (x + 0.044715 * (x * x * x)))))


def _log1p(e):
    u = 1.0 + e
    return jnp.where(u == 1.0, e, jnp.log(u) * (e / (u - 1.0)))


def _softplus(x):
    return jnp.maximum(x, 0.0) + _log1p(jnp.exp(-jnp.abs(x)))


def _row_iota(n):
    return lax.broadcasted_iota(jnp.int32, (n, 1), 0)


def _row_time(p):
    return (p & (SUBLANES - 1)) * LRU_SEG + (p >> 3)


def _embed_kernel(x_ref, meta_ref, g_ref, b_ref, o_ref):
    c = pl.program_id(1)
    keep = jnp.logical_or(c > 0, _row_iota(Q) >= PAD)
    for i in range(x_ref.shape[0]):
        src = jnp.where(c == 0, meta_ref[...], x_ref[i])
        o_ref[i] = jnp.where(keep, _ln(src, g_ref[...], b_ref[...]), 0.0)


def _embed(x, meta_pad, g, b):
    bsz, t, d = x.shape
    nc = t // Q + 1
    nb = math.gcd(bsz, 8)
    return pl.pallas_call(
        _embed_kernel,
        grid=(bsz // nb, nc),
        in_specs=[
            pl.BlockSpec((nb, Q, d), lambda i, c: (i, jnp.maximum(c - 1, 0), 0)),
            pl.BlockSpec((Q, d), lambda i, c: (0, 0)),
            pl.BlockSpec((1, d), lambda i, c: (0, 0)),
            pl.BlockSpec((1, d), lambda i, c: (0, 0)),
        ],
        out_specs=pl.BlockSpec((nb, Q, d), lambda i, c: (i, c, 0)),
        out_shape=jax.ShapeDtypeStruct((bsz, nc * Q, d), F32),
        compiler_params=_cparams("parallel", "parallel"),
        name="embed_ln",
    )(x, meta_pad, g, b)


def _head_onehot(offset):
    r = lax.broadcasted_iota(jnp.int32, (LANES, SSD_INNER), 0)
    c = lax.broadcasted_iota(jnp.int32, (LANES, SSD_INNER), 1)
    return jnp.where(r - offset == c // SSD_HEAD_DIM, 1.0, 0.0).astype(BF16)


def _pair_rhs(x_pair):
    lane = lax.broadcasted_iota(jnp.int32, x_pair.shape, 1)
    lo = jnp.where(lane < SSD_HEAD_DIM, x_pair, 0.0)
    hi = jnp.where(lane >= SSD_HEAD_DIM, x_pair, 0.0)
    return jnp.concatenate([lo, hi], axis=0)


def _ssd_decay(dt_raw, dtb, alog, first):
    lane = lax.broadcasted_iota(jnp.int32, (1, LANES), 1)
    dt = _softplus(dt_raw + dtb)
    dt = jnp.where(jnp.logical_or(jnp.logical_not(first), _row_time(_row_iota(Q)) >= PAD), dt, 0.0)
    a = dt * (-jnp.exp(alog))
    ti = _row_time(lax.broadcasted_iota(jnp.int32, (Q, Q), 0))
    si = _row_time(lax.broadcasted_iota(jnp.int32, (Q, Q), 1))
    lower = jnp.where(si <= ti, 1.0, 0.0).astype(BF16)
    upper = jnp.where(si >= ti, 1.0, 0.0).astype(BF16)
    cs = jnp.where(lane < SSD_HEADS, _split_dot(a, lower, 3, True), _split_dot(a, upper, 3, True))
    edge = jnp.where(lane < SSD_HEADS, cs[Q - 1:Q, :], cs[0:1, :])
    wsrc = dt * jnp.exp(edge - cs)
    ecs = jnp.exp(cs)
    return dict(
        cs=cs, ecs=ecs, cs_t=cs.T, dt_t=dt.T, lt=si < ti, eq=si == ti,
        wf=_split_dot(wsrc, _head_onehot(0), 2), wb=_split_dot(wsrc, _head_onehot(SSD_HEADS), 2),
        dec=_split_dot(jnp.broadcast_to(ecs[Q - 1:Q, :], (8, LANES)), _head_onehot(0), 2)[0:1, :])


def _ssd_forward(xs, bm, cm, dk, sf_scr):
    cs, ecs, cs_t, dt_t, lt, eq = dk["cs"], dk["ecs"], dk["cs_t"], dk["dt_t"], dk["lt"], dk["eq"]
    sf = sf_scr[...]
    sf_b = sf.astype(BF16)
    xs_b = xs.astype(BF16)
    ys = []
    for j in range(SSD_HEADS // 2):
        g = (2 * j) // (SSD_HEADS // SSD_GROUPS)
        c_g = cm[:, g * SSD_STATE:(g + 1) * SSD_STATE]
        b_g = bm[:, g * SSD_STATE:(g + 1) * SSD_STATE]
        gmat = _dot_nt(c_g, b_g)
        lhs = []
        for h in (2 * j, 2 * j + 1):
            hb = SSD_HEADS + h
            e = jnp.where(lt, cs[:, h:h + 1] - cs_t[h:h + 1, :], cs[:, hb:hb + 1] - cs_t[hb:hb + 1, :])
            dsel = (jnp.where(lt, dt_t[h:h + 1, :], dt_t[hb:hb + 1, :])
                    + jnp.where(eq, dt_t[h:h + 1, :], 0.0))
            lhs.append((gmat * jnp.exp(e) * dsel).astype(BF16))
        for h in (2 * j, 2 * j + 1):
            lhs.append((c_g * ecs[:, h:h + 1]).astype(BF16))
        lhs = jnp.concatenate(lhs, axis=1)
        sl = slice(2 * j * SSD_HEAD_DIM, (2 * j + 2) * SSD_HEAD_DIM)
        rhs = jnp.concatenate([_pair_rhs(xs_b[:, sl]), _pair_rhs(sf_b[:, sl])], axis=0)
        ys.append(jnp.dot(lhs, rhs, preferred_element_type=F32))
    y_part = jnp.concatenate(ys, axis=1)

    xf = (xs * dk["wf"]).astype(BF16)
    xb = (xs * dk["wb"]).astype(BF16)
    st_f = []
    st_b = []
    for g in range(SSD_GROUPS):
        b_gt = bm[:, g * SSD_STATE:(g + 1) * SSD_STATE].T.astype(BF16)
        sl = slice(g * 256, (g + 1) * 256)
        st_f.append(jnp.dot(b_gt, xf[:, sl], preferred_element_type=F32))
        st_b.append(jnp.dot(b_gt, xb[:, sl], preferred_element_type=F32))
    sf_scr[...] = sf * dk["dec"] + jnp.concatenate(st_f, axis=1)
    return y_part, jnp.concatenate(st_b, axis=1)


def _ssd_reverse(cm, ecs, sb_chunk, sb_scr):
    sb = sb_scr[...]
    sb_b = sb.astype(BF16)
    ys = []
    for j in range(SSD_HEADS // 2):
        g = (2 * j) // (SSD_HEADS // SSD_GROUPS)
        c_g = cm[:, g * SSD_STATE:(g + 1) * SSD_STATE]
        lhs = [(c_g * ecs[:, SSD_HEADS + h:SSD_HEADS + h + 1]).astype(BF16) for h in (2 * j, 2 * j + 1)]
        sl = slice(2 * j * SSD_HEAD_DIM, (2 * j + 2) * SSD_HEAD_DIM)
        ys.append(jnp.dot(jnp.concatenate(lhs, axis=1), _pair_rhs(sb_b[:, sl]), preferred_element_type=F32))
    dec = _split_dot(jnp.broadcast_to(ecs[0:1, :], (8, LANES)), _head_onehot(SSD_HEADS), 2)[0:1, :]
    sb_scr[...] = sb * dec + sb_chunk
    return jnp.concatenate(ys, axis=1)


LRU_TILE = 4 * LRU_BLOCK
LRU_TILES = LRU_WIDTH // LRU_TILE


def _lru_gates(xc, wg, brg, big, lam):
    pre = jnp.dot(xc.astype(BF16), wg, preferred_element_type=F32)
    r = _sigmoid(pre[:, :LRU_TILE] + brg)
    i = _sigmoid(pre[:, LRU_TILE:] + big)
    log_a = (-LRU_C) * r * _softplus(-lam)
    a = jnp.exp(log_a)
    u = jnp.sqrt(-jnp.tanh(log_a) * (a * a + 1.0)) * (i * xc)
    return a, u


def _lru_scan(a, u, p_scr, o_scr, h_scr, cols, reverse):
    order = range(LRU_SEG - 1, -1, -1) if reverse else range(LRU_SEG)
    h = jnp.zeros((SUBLANES, a.shape[1]), F32)
    p = jnp.ones((SUBLANES, a.shape[1]), F32)
    for j in order:
        rows = slice(j * SUBLANES, (j + 1) * SUBLANES)
        h = a[rows] * h + u[rows]
        p = a[rows] * p
        o_scr[rows, cols] = h
        p_scr[rows, cols] = p
    carry = h_scr[:, cols]
    enter = [None] * SUBLANES
    for s in (range(SUBLANES - 1, -1, -1) if reverse else range(SUBLANES)):
        enter[s] = carry
        carry = h[s:s + 1, :] + p[s:s + 1, :] * carry
    h_scr[:, cols] = carry
    enter = jnp.concatenate(enter, axis=0)
    for j in order:
        rows = slice(j * SUBLANES, (j + 1) * SUBLANES)
        o_scr[rows, cols] = o_scr[rows, cols] + p_scr[rows, cols] * enter


def _lru_tile(j, xc, live, wg_ref, brg_ref, big_ref, lam_ref, p_scr, o_scr, h_scr, reverse):
    cols = slice(j * LRU_TILE, (j + 1) * LRU_TILE)
    a, u = _lru_gates(xc, wg_ref[j], brg_ref[:, cols], big_ref[:, cols], lam_ref[:, cols])
    if live is not None:
        u = jnp.where(live, u, 0.0)
    _lru_scan(a, u, p_scr, o_scr, h_scr, cols, reverse)


def _interleaved_rows(slab_scr, first, count, pitch):
    groups = [jnp.concatenate([slab_scr[s, pl.ds(first + g, SUBLANES, stride=pitch), :]
                               for s in range(slab_scr.shape[0])], axis=1) for g in range(count)]
    return jnp.concatenate(groups, axis=0)


def _to_segment_slabs(x, slab_scr, rows, pitch):
    for k in range(slab_scr.shape[0]):
        for s in range(SUBLANES):
            slab_scr[k, pitch * s:pitch * s + rows, :] = x[LRU_SEG * s:LRU_SEG * s + rows,
                                                           k * LANES:(k + 1) * LANES]


def _to_slabs(x, slab_scr):
    for s in range(slab_scr.shape[0]):
        slab_scr[s, 0:x.shape[0], :] = x[:, s * LANES:(s + 1) * LANES]


def _dwconv4(pe, cw, cb):
    acc = cb
    for k in range(4):
        acc = acc + pe[k * SUBLANES:k * SUBLANES + Q] * cw[k:k + 1, :]
    return acc


PK_XS, PK_CM, PK_YP, PK_XC, PK_HF = 0, SSD_INNER, SSD_INNER + BC_DIM, 2 * SSD_INNER + BC_DIM, 2 * SSD_INNER + BC_DIM + LRU_WIDTH
PK_WIDTH = PK_HF + LRU_WIDTH


def _fwd_kernel(hp_ref, hc_ref, hn_ref, wf_ref, scw_ref, scb_ref, dtb_ref, alog_ref,
                lcw_ref, lcb_ref, wg_ref, brg_ref, big_ref, lam_ref,
                pk_o, sb_o, aux_o,
                sf_scr, ext_scr, p_scr, hrow_scr, h_scr):
    c = pl.program_id(1)
    nc = pl.num_programs(1)
    first = c == 0

    @pl.when(first)
    def _():
        sf_scr[...] = jnp.zeros_like(sf_scr)
        h_scr[...] = jnp.zeros_like(h_scr)

    def head(i):
        nxt = hn_ref[i] * jnp.where(c < nc - 1, 1.0, 0.0)
        _to_segment_slabs(jnp.concatenate([hp_ref[i], hc_ref[i], nxt], axis=0), ext_scr.at[i],
                          LRU_SEG + 2 * HALO, FWD_PITCH)
        xl = _interleaved_rows(ext_scr.at[i], HALO - 2, LRU_SEG + 3, FWD_PITCH).astype(BF16)
        pe_dt = jnp.dot(xl[2 * SUBLANES:2 * SUBLANES + Q], wf_ref[:, XBC_DIM + LRU_WIDTH:],
                        preferred_element_type=F32)
        pe_l = jnp.dot(xl, wf_ref[:, XBC_DIM:XBC_DIM + LRU_WIDTH], preferred_element_type=F32)
        dk = _ssd_decay(pe_dt, dtb_ref[...], alog_ref[...], first)
        aux_o[i] = dk["ecs"]
        return xl, pe_l, dk

    def lru(i, xl, pe_l):
        pk = pk_o.at[i]
        live = jnp.logical_or(jnp.logical_not(first), _row_time(_row_iota(Q)) >= PAD)
        pe_s = []
        for j in range(LRU_TILES):
            cols = slice(j * LRU_TILE, (j + 1) * LRU_TILE)
            pe_s.append(jnp.dot(xl, wf_ref[:, 2 * j * LRU_TILE:2 * (j + 1) * LRU_TILE],
                                preferred_element_type=F32))
            xc = _dwconv4(pe_l[:, cols], lcw_ref[:, cols], lcb_ref[:, cols])
            pk[:, PK_XC + j * LRU_TILE:PK_XC + (j + 1) * LRU_TILE] = xc.astype(pk.dtype)
            _lru_tile(j, xc, live, wg_ref, brg_ref, big_ref, lam_ref, p_scr.at[i], hrow_scr.at[i],
                      h_scr.at[i], reverse=False)
        pk[:, PK_HF:] = hrow_scr[i].astype(pk.dtype)
        return jnp.concatenate(pe_s[:LRU_TILES // 2], axis=1), jnp.concatenate(pe_s[LRU_TILES // 2:], axis=1)

    def ssd(i, pe_x, pe_bc, dk):
        pk = pk_o.at[i]
        xs = _silu(_dwconv4(pe_x, scw_ref[:, :SSD_INNER], scb_ref[:, :SSD_INNER]))
        bc = _silu(_dwconv4(pe_bc, scw_ref[:, SSD_INNER:], scb_ref[:, SSD_INNER:]))
        pk[:, PK_XS:PK_CM] = xs.astype(pk.dtype)
        pk[:, PK_CM:PK_YP] = bc[:, BC_DIM:].astype(pk.dtype)
        y_part, st_b = _ssd_forward(xs, bc[:, :BC_DIM], bc[:, BC_DIM:], dk, sf_scr.at[i])
        pk[:, PK_YP:PK_XC] = y_part.astype(pk.dtype)
        sb_o[i] = st_b.astype(sb_o.dtype)

    xl, pe_l, dk = head(0)
    pe_x, pe_bc = lru(0, xl, pe_l)
    for i in range(1, SEQ_PER_STEP):
        xl, pe_l, dk_next = head(i)
        ssd(i - 1, pe_x, pe_bc, dk)
        pe_x, pe_bc = lru(i, xl, pe_l)
        dk = dk_next
    ssd(SEQ_PER_STEP - 1, pe_x, pe_bc, dk)


def _fwd(h, lw):
    bsz, lp, d = h.shape
    nc = lp // Q
    nh = lp // HALO
    per = Q // HALO
    w = LRU_WIDTH
    nb = SEQ_PER_STEP
    seq = lambda n: pl.BlockSpec((nb, Q, n), lambda i, c: (i, c, 0))
    return pl.pallas_call(
        _fwd_kernel,
        grid=(bsz // nb, nc),
        in_specs=[
            pl.BlockSpec((nb, HALO, d), lambda i, c: (i, jnp.maximum(c * per - 1, 0), 0)),
            seq(d),
            pl.BlockSpec((nb, HALO, d), lambda i, c: (i, jnp.minimum((c + 1) * per, nh - 1), 0)),
            _const_spec((d, XBC_DIM + w + LANES)),
            _const_spec((4, XBC_DIM)), _const_spec((1, XBC_DIM)), _const_spec((1, LANES)), _const_spec((1, LANES)),
            _const_spec((4, w)), _const_spec((1, w)), _const_spec((w // 256, 256, 512)),
            _const_spec((1, w)), _const_spec((1, w)), _const_spec((1, w)),
        ],
        out_specs=[
            seq(PK_WIDTH),
            pl.BlockSpec((nb, None, SSD_STATE, SSD_INNER), lambda i, c: (i, c, 0, 0)),
            seq(LANES),
        ],
        out_shape=[
            jax.ShapeDtypeStruct((bsz, lp, PK_WIDTH), ACT_DT),
            jax.ShapeDtypeStruct((bsz, nc, SSD_STATE, SSD_INNER), ACT_DT),
            jax.ShapeDtypeStruct((bsz, lp, LANES), F32),
        ],
        scratch_shapes=[pltpu.VMEM((nb, SSD_STATE, SSD_INNER), F32),
                        pltpu.VMEM((nb, d // LANES, SUBLANES * FWD_PITCH, LANES), F32),
                        pltpu.VMEM((nb, Q, w), F32), pltpu.VMEM((nb, Q, w), F32), pltpu.VMEM((nb, 1, w), F32)],
        compiler_params=_cparams("parallel", "arbitrary"),
        name="fwd_sweep",
    )(h, h, h, lw["w_fwd"], lw["ssd_conv_w"], lw["ssd_conv_b"], lw["dt_bias"], lw["a_log"],
      lw["lru_conv_w"], lw["lru_conv_b"], lw["wg"][0], lw["brg"][0], lw["big"][0], lw["lam"][0])


def _rev_kernel(h_ref, pk_ref, sb_ref, aux_ref,
                wr_ref, dskip_ref, ng_ref, wso_ref, wg_ref, brg_ref, big_ref, lam_ref, wlo_ref,
                wo_ref, g_ref, b_ref, o_ref,
                sb_scr, slab_scr, p_scr, hrow_scr, h_scr):
    @pl.when(pl.program_id(1) == 0)
    def _():
        sb_scr[...] = jnp.zeros_like(sb_scr)
        h_scr[...] = jnp.zeros_like(h_scr)

    k = pl.program_id(1)
    nc = pl.num_programs(1)

    def head(i):
        _to_segment_slabs(h_ref[i], slab_scr.at[i], LRU_SEG, REV_PITCH)
        return _interleaved_rows(slab_scr.at[i], 0, LRU_SEG, REV_PITCH)

    def body(i, h):
        pk = pk_ref.at[i]
        hb = h.astype(BF16)
        zl = []
        for j in range(LRU_TILES):
            zl.append(jnp.dot(hb, wr_ref[:, 2 * j * LRU_TILE:2 * (j + 1) * LRU_TILE],
                              preferred_element_type=F32))
            xc = pk[:, PK_XC + j * LRU_TILE:PK_XC + (j + 1) * LRU_TILE].astype(F32)
            _lru_tile(j, xc, None, wg_ref, brg_ref, big_ref, lam_ref, p_scr.at[i], hrow_scr.at[i],
                      h_scr.at[i], reverse=True)
        z = jnp.concatenate(zl[:LRU_TILES // 2], axis=1)
        lg = jnp.concatenate(zl[LRU_TILES // 2:], axis=1)

        y_off = _ssd_reverse(pk[:, PK_CM:PK_YP].astype(F32), aux_ref[i], sb_ref[i].astype(F32), sb_scr.at[i])
        y = pk[:, PK_YP:PK_XC].astype(F32) + y_off + pk[:, PK_XS:PK_CM].astype(F32) * dskip_ref[...]
        y = y * _silu(z)
        y = y * lax.rsqrt(jnp.mean(y * y, -1, keepdims=True) + RMS_EPS) * ng_ref[...]
        y_s = _dot(y, wso_ref[...])
        mg = jnp.dot(hb, wr_ref[:, SSD_INNER + LRU_WIDTH:], preferred_element_type=F32)
        y_l = _dot((pk[:, PK_HF:].astype(F32) + hrow_scr[i]) * _gelu(lg), wlo_ref[...])
        return _sigmoid(mg[:, :D_MODEL]) * y_s + _sigmoid(mg[:, D_MODEL:]) * y_l

    def tail(i, h, mix):
        out = _ln(ALPHA * h + _dot(mix, wo_ref[...]), g_ref[...], b_ref[...])
        out = jnp.where(jnp.logical_or(k < nc - 1, _row_time(_row_iota(Q)) >= PAD), out, 0.0)
        _to_slabs(out, slab_scr.at[i])
        per_seg = LRU_SEG // SUBLANES
        for r in range(Q // SUBLANES):
            start = (r % per_seg) * SUBLANES * SUBLANES + r // per_seg
            for s in range(slab_scr.shape[1]):
                o_ref[i, r * SUBLANES:(r + 1) * SUBLANES, s * LANES:(s + 1) * LANES] = (
                    slab_scr[i, s, pl.ds(start, SUBLANES, stride=SUBLANES), :])

    h = head(0)
    mix = body(0, h)
    for i in range(1, SEQ_PER_STEP):
        h_next = head(i)
        tail(i - 1, h, mix)
        h = h_next
        mix = body(i, h)
    tail(SEQ_PER_STEP - 1, h, mix)


def _rev(h, pk, sb, aux, lw):
    bsz, lp, d = h.shape
    nc = lp // Q
    w = LRU_WIDTH
    nb = SEQ_PER_STEP
    seq = lambda n: pl.BlockSpec((nb, Q, n), lambda i, c: (i, nc - 1 - c, 0))
    return pl.pallas_call(
        _rev_kernel,
        grid=(bsz // nb, nc),
        in_specs=[
            seq(d), seq(PK_WIDTH),
            pl.BlockSpec((nb, None, SSD_STATE, SSD_INNER), lambda i, c: (i, nc - 1 - c, 0, 0)),
            seq(LANES),
            _const_spec((d, SSD_INNER + w + 2 * d)),
            _const_spec((1, SSD_INNER)), _const_spec((1, SSD_INNER)), _const_spec((SSD_INNER, d)),
            _const_spec((w // 256, 256, 512)), _const_spec((1, w)), _const_spec((1, w)), _const_spec((1, w)),
            _const_spec((w, d)), _const_spec((d, d)), _const_spec((1, d)), _const_spec((1, d)),
        ],
        out_specs=seq(d),
        out_shape=jax.ShapeDtypeStruct((bsz, lp, d), F32),
        scratch_shapes=[pltpu.VMEM((nb, SSD_STATE, SSD_INNER), F32),
                        pltpu.VMEM((nb, d // LANES, SUBLANES * REV_PITCH, LANES), F32),
                        pltpu.VMEM((nb, Q, w), F32), pltpu.VMEM((nb, Q, w), F32), pltpu.VMEM((nb, 1, w), F32)],
        compiler_params=_cparams("parallel", "arbitrary"),
        name="rev_sweep",
    )(h, pk, sb, aux, lw["w_rev"], lw["dskip"], lw["norm_g"], lw["w_ssd_out"],
      lw["wg"][1], lw["brg"][1], lw["big"][1], lw["lam"][1], lw["w_lru_out"], lw["w_o"],
      lw["ln1_g"], lw["ln1_b"])


def _ffn_kernel(hp_ref, hc_ref, hn_ref, wu_ref, cw_ref, cb_ref, wd_ref, g_ref, b_ref, o_ref,
                ext_scr, out_scr):
    c = pl.program_id(1)
    nc = pl.num_programs(1)
    tf = hc_ref.shape[0]
    nseg = tf // SEG
    slabs = D_MODEL // LANES
    keep_next = jnp.where(c < nc - 1, 1.0, 0.0)
    for s in range(slabs):
        ls = slice(s * LANES, (s + 1) * LANES)
        ext_scr[s, 0:HALO, :] = hp_ref[:, ls]
        ext_scr[s, HALO:HALO + tf, :] = hc_ref[:, ls]
        ext_scr[s, HALO + tf:, :] = hn_ref[:, ls] * keep_next

    def group(g):
        halves = [jnp.concatenate([ext_scr[s, pl.ds(HALO + g + k * SUBLANES * nseg, SUBLANES, stride=nseg), :]
                                   for s in range(slabs)], axis=1) for k in range(SEG // SUBLANES)]
        return jnp.concatenate(halves, axis=0)
    xperm = jnp.concatenate([group(g) for g in range(-1, nseg + 1)], axis=0)
    xb = xperm.astype(BF16)

    def conv3(up, col):
        cw = cw_ref[:, pl.ds(col, FFN_COLS)].astype(BF16)
        acc = cb_ref[:, pl.ds(col, FFN_COLS)].astype(BF16) + up[0:tf] * cw[0:1, :]
        acc = acc + up[SEG:SEG + tf] * cw[1:2, :]
        return acc + up[2 * SEG:2 * SEG + tf] * cw[2:3, :]

    f = None
    for j in range(D_FF // FFN_COLS):
        cg, cv = j * FFN_COLS, D_FF + j * FFN_COLS
        ug = conv3(jnp.dot(xb, wu_ref[:, pl.ds(cg, FFN_COLS)], preferred_element_type=F32).astype(BF16), cg)
        uv = conv3(jnp.dot(xb, wu_ref[:, pl.ds(cv, FFN_COLS)], preferred_element_type=F32).astype(BF16), cv)
        part = jnp.dot(_gelu(ug) * uv, wd_ref[pl.ds(cg, FFN_COLS), :], preferred_element_type=F32)
        f = part if f is None else f + part
    y = _ln(ALPHA * xperm[SEG:SEG + tf] + f, g_ref[...], b_ref[...])
    p = _row_iota(tf)
    t = c * tf + (p & (SEG - 1)) * nseg + (p >> SEG_SHIFT)
    y = jnp.where(t >= PAD, y, 0.0)
    for j in range(nseg):
        for k in range(SEG // SUBLANES):
            r0 = j * SEG + k * SUBLANES
            for s in range(slabs):
                out_scr[s, pl.ds(j + k * SUBLANES * nseg, SUBLANES, stride=nseg), :] = (
                    y[r0:r0 + SUBLANES, s * LANES:(s + 1) * LANES])
    for s in range(slabs):
        o_ref[:, s * LANES:(s + 1) * LANES] = out_scr[s]


def _ffn_tile(lp):
    for parts in (4, 8, 2, 1):
        if lp % (parts * SEG) == 0 and lp // parts <= 640:
            return lp // parts
    return Q


def _ffn(h, lw):
    bsz, lp, d = h.shape
    tf = _ffn_tile(lp)
    nh = lp // HALO
    per = tf // HALO
    seq = pl.BlockSpec((None, tf, d), lambda i, c: (i, c, 0))
    return pl.pallas_call(
        _ffn_kernel,
        grid=(bsz, lp // tf),
        in_specs=[
            pl.BlockSpec((None, HALO, d), lambda i, c: (i, jnp.maximum(c * per - 1, 0), 0)),
            seq,
            pl.BlockSpec((None, HALO, d), lambda i, c: (i, jnp.minimum((c + 1) * per, nh - 1), 0)),
            _const_spec((d, 2 * D_FF)), _const_spec((3, 2 * D_FF)), _const_spec((1, 2 * D_FF)),
            _const_spec((D_FF, d)), _const_spec((1, d)), _const_spec((1, d)),
        ],
        out_specs=seq,
        out_shape=jax.ShapeDtypeStruct((bsz, lp, d), F32),
        scratch_shapes=[pltpu.VMEM((d // LANES, tf + 2 * HALO, LANES), F32),
                        pltpu.VMEM((d // LANES, tf, LANES), F32)],
        compiler_params=_cparams("parallel", "parallel"),
        name="ffn_ln",
    )(h, h, h, lw["w_up"], lw["ffn_conv_w"], lw["ffn_conv_b"], lw["w_down"], lw["ln2_g"], lw["ln2_b"])


def _pad_lanes(v):
    return jnp.pad(v.reshape(1, -1).astype(F32), ((0, 0), (0, LANES - v.size)))


def _gate_tiles(w_rg, w_ig):
    def tiles(w):
        w4 = w.reshape(LRU_HEADS // 4, 4, LRU_BLOCK, LRU_BLOCK)
        eye = jnp.eye(4, dtype=w.dtype)
        return jnp.einsum("thij,hk->thikj", w4, eye).reshape(LRU_HEADS // 4, 256, 256)
    return jnp.concatenate([tiles(w_rg), tiles(w_ig)], axis=-1).astype(BF16)


def _prep_layer(p, li):
    w_in = p["w_in"][li]
    row = lambda v: v.reshape(1, -1).astype(F32)
    w_dt = jnp.pad(w_in[:, S2:S3], ((0, 0), (0, LANES - (S3 - S2))))
    return dict(
        w_fwd=jnp.concatenate([w_in[:, S1:S2], w_in[:, S4:S5], w_dt], axis=1).astype(BF16),
        w_rev=jnp.concatenate([w_in[:, :S1], w_in[:, S3:S4], w_in[:, S5:]], axis=1).astype(BF16),
        ssd_conv_w=p["ssd_conv_w"][li], ssd_conv_b=row(p["ssd_conv_b"][li]),
        dt_bias=_pad_lanes(p["ssd_dt_bias"][li]), a_log=_pad_lanes(p["ssd_a_log"][li]),
        dskip=row(jnp.repeat(p["ssd_d"][li], SSD_HEAD_DIM)), norm_g=row(p["ssd_norm_g"][li]),
        w_ssd_out=p["w_ssd_out"][li].astype(BF16),
        lru_conv_w=p["lru_conv_w"][li], lru_conv_b=row(p["lru_conv_b"][li]),
        wg=[_gate_tiles(p["lru_w_rg"][li, d], p["lru_w_ig"][li, d]) for d in range(2)],
        brg=[row(p["lru_b_rg"][li, d]) for d in range(2)],
        big=[row(p["lru_b_ig"][li, d]) for d in range(2)],
        lam=[row(p["lru_lambda"][li, d]) for d in range(2)],
        w_lru_out=p["w_lru_out"][li].astype(BF16), w_o=p["w_o"][li].astype(BF16),
        ln1_g=row(p["ln1_g"][li]), ln1_b=row(p["ln1_b"][li]),
        w_up=p["w_up"][li].astype(BF16), ffn_conv_w=p["ffn_conv_w"][li],
        ffn_conv_b=row(p["ffn_conv_b"][li]), w_down=p["w_down"][li].astype(BF16),
        ln2_g=row(p["ln2_g"][li]), ln2_b=row(p["ln2_b"][li]),
    )


def _layer(h, lw):
    pk, sb, aux = _fwd(h, lw)
    h = _rev(h, pk, sb, aux, lw)
    return _ffn(h, lw)


def _run(x, meta_pad, ln_g, ln_b, layers):
    h = _embed(x, meta_pad, ln_g, ln_b)
    for lw in layers:
        h = _layer(h, lw)
    return h[:, Q:]


def kernel(x_prompt, x_sample, meta, ln_in_g, ln_in_b, w_in, ssd_conv_w, ssd_conv_b, ssd_dt_bias, ssd_a_log, ssd_d, ssd_norm_g, w_ssd_out, lru_conv_w, lru_conv_b, lru_w_rg, lru_b_rg, lru_w_ig, lru_b_ig, lru_lambda, w_lru_out, w_o, ln1_g, ln1_b, w_up, ffn_conv_w, ffn_conv_b, w_down, ln2_g, ln2_b):
    p = dict(w_in=w_in, ssd_conv_w=ssd_conv_w, ssd_conv_b=ssd_conv_b, ssd_dt_bias=ssd_dt_bias,
             ssd_a_log=ssd_a_log, ssd_d=ssd_d, ssd_norm_g=ssd_norm_g, w_ssd_out=w_ssd_out,
             lru_conv_w=lru_conv_w, lru_conv_b=lru_conv_b, lru_w_rg=lru_w_rg, lru_b_rg=lru_b_rg,
             lru_w_ig=lru_w_ig, lru_b_ig=lru_b_ig, lru_lambda=lru_lambda, w_lru_out=w_lru_out,
             w_o=w_o, ln1_g=ln1_g, ln1_b=ln1_b, w_up=w_up, ffn_conv_w=ffn_conv_w,
             ffn_conv_b=ffn_conv_b, w_down=w_down, ln2_g=ln2_g, ln2_b=ln2_b)
    layers = [_prep_layer(p, li) for li in range(w_in.shape[0])]
    meta_pad = jnp.pad(meta.astype(F32), ((PAD, 0), (0, 0)))
    g = ln_in_g.reshape(1, -1)
    b = ln_in_b.reshape(1, -1)
    return (_run(x_prompt, meta_pad, g, b, layers), _run(x_sample, meta_pad, g, b, layers))
```

```python
import math

import jax
import jax.numpy as jnp
from jax import lax
from jax.experimental import pallas as pl
from jax.experimental.pallas import tpu as pltpu

F32 = jnp.float32
BF16 = jnp.bfloat16

D_MODEL = 1024
DEPTH = 4
N_META = 16
SSD_HEADS = 16
SSD_HEAD_DIM = 64
SSD_INNER = SSD_HEADS * SSD_HEAD_DIM
SSD_GROUPS = 4
SSD_STATE = 128
BC_DIM = SSD_GROUPS * SSD_STATE
XBC_DIM = SSD_INNER + 2 * BC_DIM
LRU_WIDTH = 1024
LRU_HEADS = 16
LRU_BLOCK = LRU_WIDTH // LRU_HEADS
LRU_C = 8.0
D_FF = 3 * D_MODEL
S1 = SSD_INNER
S2 = S1 + XBC_DIM
S3 = S2 + 2 * SSD_HEADS
S4 = S3 + LRU_WIDTH
S5 = S4 + LRU_WIDTH
ALPHA = (2 * DEPTH) ** 0.25
LN_EPS = 1e-5
RMS_EPS = 1e-6

Q = 128
PAD = Q - N_META
HALO = 8
LANES = 128
SUBLANES = 8
LRU_SEG = Q // SUBLANES
FWD_PITCH = LRU_SEG + 2 * HALO + 8
REV_PITCH = LRU_SEG + 8
SEG_SHIFT = 4
SEG = 1 << SEG_SHIFT
FFN_COLS = 512
ACT_DT = BF16
SEQ_PER_STEP = 4
VMEM_LIMIT = 56 * 1024 * 1024


def _cparams(*sem):
    return pltpu.CompilerParams(dimension_semantics=sem, vmem_limit_bytes=VMEM_LIMIT)


def _const_spec(shape):
    zeros = (0,) * len(shape)
    return pl.BlockSpec(shape, lambda i, c: zeros, pipeline_mode=pl.Buffered(1))


def _dot(a, b):
    return jnp.dot(a.astype(BF16), b.astype(BF16), preferred_element_type=F32)


def _dot_nt(a, b):
    return lax.dot_general(a.astype(BF16), b.astype(BF16), (((1,), (1,)), ((), ())),
                           preferred_element_type=F32)


def _split_dot(v, e, terms, e_left=False):
    acc = None
    r = v
    for _ in range(terms):
        hi = r.astype(BF16)
        part = (jnp.dot(e, hi, preferred_element_type=F32) if e_left
                else jnp.dot(hi, e, preferred_element_type=F32))
        acc = part if acc is None else acc + part
        r = r - hi.astype(F32)
    return acc


def _ln(x, g, b):
    mu = jnp.mean(x, -1, keepdims=True)
    xc = x - mu
    var = jnp.mean(xc * xc, -1, keepdims=True)
    return xc * lax.rsqrt(var + LN_EPS) * g + b


def _sigmoid(x):
    return 1.0 / (1.0 + jnp.exp(-x))


def _silu(x):
    return x * _sigmoid(x)


def _gelu(x):
    c = math.sqrt(2.0 / math.pi)
    return x * (0.5 * (1.0 + jnp.tanh(c * (x + 0.044715 * (x * x * x)))))


def _log1p(e):
    u = 1.0 + e
    return jnp.where(u == 1.0, e, jnp.log(u) * (e / (u - 1.0)))


def _softplus(x):
    return jnp.maximum(x, 0.0) + _log1p(jnp.exp(-jnp.abs(x)))


def _row_iota(n):
    return lax.broadcasted_iota(jnp.int32, (n, 1), 0)


def _row_time(p):
    return (p & (SUBLANES - 1)) * LRU_SEG + (p >> 3)


def _embed_kernel(x_ref, meta_ref, g_ref, b_ref, o_ref):
    c = pl.program_id(1)
    keep = jnp.logical_or(c > 0, _row_iota(Q) >= PAD)
    for i in range(x_ref.shape[0]):
        src = jnp.where(c == 0, meta_ref[...], x_ref[i])
        o_ref[i] = jnp.where(keep, _ln(src, g_ref[...], b_ref[...]), 0.0)


def _embed(x, meta_pad, g, b):
    bsz, t, d = x.shape
    nc = t // Q + 1
    nb = math.gcd(bsz, 8)
    return pl.pallas_call(
        _embed_kernel,
        grid=(bsz // nb, nc),
        in_specs=[
            pl.BlockSpec((nb, Q, d), lambda i, c: (i, jnp.maximum(c - 1, 0), 0)),
            pl.BlockSpec((Q, d), lambda i, c: (0, 0)),
            pl.BlockSpec((1, d), lambda i, c: (0, 0)),
            pl.BlockSpec((1, d), lambda i, c: (0, 0)),
        ],
        out_specs=pl.BlockSpec((nb, Q, d), lambda i, c: (i, c, 0)),
        out_shape=jax.ShapeDtypeStruct((bsz, nc * Q, d), F32),
        compiler_params=_cparams("parallel", "parallel"),
        name="embed_ln",
    )(x, meta_pad, g, b)


def _head_onehot(offset):
    r = lax.broadcasted_iota(jnp.int32, (LANES, SSD_INNER), 0)
    c = lax.broadcasted_iota(jnp.int32, (LANES, SSD_INNER), 1)
    return jnp.where(r - offset == c // SSD_HEAD_DIM, 1.0, 0.0).astype(BF16)


def _pair_rhs(x_pair):
    lane = lax.broadcasted_iota(jnp.int32, x_pair.shape, 1)
    lo = jnp.where(lane < SSD_HEAD_DIM, x_pair, 0.0)
    hi = jnp.where(lane >= SSD_HEAD_DIM, x_pair, 0.0)
    return jnp.concatenate([lo, hi], axis=0)


def _ssd_decay(dt_raw, dtb, alog, first):
    lane = lax.broadcasted_iota(jnp.int32, (1, LANES), 1)
    dt = _softplus(dt_raw + dtb)
    dt = jnp.where(jnp.logical_or(jnp.logical_not(first), _row_time(_row_iota(Q)) >= PAD), dt, 0.0)
    a = dt * (-jnp.exp(alog))
    ti = _row_time(lax.broadcasted_iota(jnp.int32, (Q, Q), 0))
    si = _row_time(lax.broadcasted_iota(jnp.int32, (Q, Q), 1))
    lower = jnp.where(si <= ti, 1.0, 0.0).astype(BF16)
    upper = jnp.where(si >= ti, 1.0, 0.0).astype(BF16)
    cs = jnp.where(lane < SSD_HEADS, _split_dot(a, lower, 3, True), _split_dot(a, upper, 3, True))
    edge = jnp.where(lane < SSD_HEADS, cs[Q - 1:Q, :], cs[0:1, :])
    wsrc = dt * jnp.exp(edge - cs)
    ecs = jnp.exp(cs)
    return dict(
        cs=cs, ecs=ecs, cs_t=cs.T, dt_t=dt.T, lt=si < ti, eq=si == ti,
        wf=_split_dot(wsrc, _head_onehot(0), 2), wb=_split_dot(wsrc, _head_onehot(SSD_HEADS), 2),
        dec=_split_dot(jnp.broadcast_to(ecs[Q - 1:Q, :], (8, LANES)), _head_onehot(0), 2)[0:1, :])


def _ssd_forward(xs, bm, cm, dk, sf_scr):
    cs, ecs, cs_t, dt_t, lt, eq = dk["cs"], dk["ecs"], dk["cs_t"], dk["dt_t"], dk["lt"], dk["eq"]
    sf = sf_scr[...]
    sf_b = sf.astype(BF16)
    xs_b = xs.astype(BF16)
    ys = []
    for j in range(SSD_HEADS // 2):
        g = (2 * j) // (SSD_HEADS // SSD_GROUPS)
        c_g = cm[:, g * SSD_STATE:(g + 1) * SSD_STATE]
        b_g = bm[:, g * SSD_STATE:(g + 1) * SSD_STATE]
        gmat = _dot_nt(c_g, b_g)
        lhs = []
        for h in (2 * j, 2 * j + 1):
            hb = SSD_HEADS + h
            e = jnp.where(lt, cs[:, h:h + 1] - cs_t[h:h + 1, :], cs[:, hb:hb + 1] - cs_t[hb:hb + 1, :])
            dsel = (jnp.where(lt, dt_t[h:h + 1, :], dt_t[hb:hb + 1, :])
                    + jnp.where(eq, dt_t[h:h + 1, :], 0.0))
            lhs.append((gmat * jnp.exp(e) * dsel).astype(BF16))
        for h in (2 * j, 2 * j + 1):
            lhs.append((c_g * ecs[:, h:h + 1]).astype(BF16))
        lhs = jnp.concatenate(lhs, axis=1)
        sl = slice(2 * j * SSD_HEAD_DIM, (2 * j + 2) * SSD_HEAD_DIM)
        rhs = jnp.concatenate([_pair_rhs(xs_b[:, sl]), _pair_rhs(sf_b[:, sl])], axis=0)
        ys.append(jnp.dot(lhs, rhs, preferred_element_type=F32))
    y_part = jnp.concatenate(ys, axis=1)

    xf = (xs * dk["wf"]).astype(BF16)
    xb = (xs * dk["wb"]).astype(BF16)
    st_f = []
    st_b = []
    for g in range(SSD_GROUPS):
        b_gt = bm[:, g * SSD_STATE:(g + 1) * SSD_STATE].T.astype(BF16)
        sl = slice(g * 256, (g + 1) * 256)
        st_f.append(jnp.dot(b_gt, xf[:, sl], preferred_element_type=F32))
        st_b.append(jnp.dot(b_gt, xb[:, sl], preferred_element_type=F32))
    sf_scr[...] = sf * dk["dec"] + jnp.concatenate(st_f, axis=1)
    return y_part, jnp.concatenate(st_b, axis=1)


def _ssd_reverse(cm, ecs, sb_chunk, sb_scr):
    sb = sb_scr[...]
    sb_b = sb.astype(BF16)
    ys = []
    for j in range(SSD_HEADS // 2):
        g = (2 * j) // (SSD_HEADS // SSD_GROUPS)
        c_g = cm[:, g * SSD_STATE:(g + 1) * SSD_STATE]
        lhs = [(c_g * ecs[:, SSD_HEADS + h:SSD_HEADS + h + 1]).astype(BF16) for h in (2 * j, 2 * j + 1)]
        sl = slice(2 * j * SSD_HEAD_DIM, (2 * j + 2) * SSD_HEAD_DIM)
        ys.append(jnp.dot(jnp.concatenate(lhs, axis=1), _pair_rhs(sb_b[:, sl]), preferred_element_type=F32))
    dec = _split_dot(jnp.broadcast_to(ecs[0:1, :], (8, LANES)), _head_onehot(SSD_HEADS), 2)[0:1, :]
    sb_scr[...] = sb * dec + sb_chunk
    return jnp.concatenate(ys, axis=1)


LRU_TILE = 4 * LRU_BLOCK
LRU_TILES = LRU_WIDTH // LRU_TILE


def _lru_gates(xc, wg, brg, big, lam):
    pre = jnp.dot(xc.astype(BF16), wg, preferred_element_type=F32)
    r = _sigmoid(pre[:, :LRU_TILE] + brg)
    i = _sigmoid(pre[:, LRU_TILE:] + big)
    log_a = (-LRU_C) * r * _softplus(-lam)
    a = jnp.exp(log_a)
    u = jnp.sqrt(-jnp.tanh(log_a) * (a * a + 1.0)) * (i * xc)
    return a, u


def _lru_scan(a, u, p_scr, o_scr, h_scr, cols, reverse):
    order = range(LRU_SEG - 1, -1, -1) if reverse else range(LRU_SEG)
    h = jnp.zeros((SUBLANES, a.shape[1]), F32)
    p = jnp.ones((SUBLANES, a.shape[1]), F32)
    for j in order:
        rows = slice(j * SUBLANES, (j + 1) * SUBLANES)
        h = a[rows] * h + u[rows]
        p = a[rows] * p
        o_scr[rows, cols] = h
        p_scr[rows, cols] = p
    carry = h_scr[:, cols]
    enter = [None] * SUBLANES
    for s in (range(SUBLANES - 1, -1, -1) if reverse else range(SUBLANES)):
        enter[s] = carry
        carry = h[s:s + 1, :] + p[s:s + 1, :] * carry
    h_scr[:, cols] = carry
    enter = jnp.concatenate(enter, axis=0)
    for j in order:
        rows = slice(j * SUBLANES, (j + 1) * SUBLANES)
        o_scr[rows, cols] = o_scr[rows, cols] + p_scr[rows, cols] * enter


def _lru_tile(j, xc, live, wg_ref, brg_ref, big_ref, lam_ref, p_scr, o_scr, h_scr, reverse):
    cols = slice(j * LRU_TILE, (j + 1) * LRU_TILE)
    a, u = _lru_gates(xc, wg_ref[j], brg_ref[:, cols], big_ref[:, cols], lam_ref[:, cols])
    if live is not None:
        u = jnp.where(live, u, 0.0)
    _lru_scan(a, u, p_scr, o_scr, h_scr, cols, reverse)


def _interleaved_rows(slab_scr, first, count, pitch):
    groups = [jnp.concatenate([slab_scr[s, pl.ds(first + g, SUBLANES, stride=pitch), :]
                               for s in range(slab_scr.shape[0])], axis=1) for g in range(count)]
    return jnp.concatenate(groups, axis=0)


def _to_segment_slabs(x, slab_scr, rows, pitch):
    for k in range(slab_scr.shape[0]):
        for s in range(SUBLANES):
            slab_scr[k, pitch * s:pitch * s + rows, :] = x[LRU_SEG * s:LRU_SEG * s + rows,
                                                           k * LANES:(k + 1) * LANES]


def _to_slabs(x, slab_scr):
    for s in range(slab_scr.shape[0]):
        slab_scr[s, 0:x.shape[0], :] = x[:, s * LANES:(s + 1) * LANES]


def _dwconv4(pe, cw, cb):
    acc = cb
    for k in range(4):
        acc = acc + pe[k * SUBLANES:k * SUBLANES + Q] * cw[k:k + 1, :]
    return acc


PK_XS, PK_CM, PK_YP, PK_XC, PK_HF = 0, SSD_INNER, SSD_INNER + BC_DIM, 2 * SSD_INNER + BC_DIM, 2 * SSD_INNER + BC_DIM + LRU_WIDTH
PK_WIDTH = PK_HF + LRU_WIDTH


def _fwd_kernel(hp_ref, hc_ref, hn_ref, wf_ref, scw_ref, scb_ref, dtb_ref, alog_ref,
                lcw_ref, lcb_ref, wg_ref, brg_ref, big_ref, lam_ref,
                pk_o, sb_o, aux_o,
                sf_scr, ext_scr, p_scr, hrow_scr, h_scr):
    c = pl.program_id(1)
    nc = pl.num_programs(1)
    first = c == 0

    @pl.when(first)
    def _():
        sf_scr[...] = jnp.zeros_like(sf_scr)
        h_scr[...] = jnp.zeros_like(h_scr)

    def head(i):
        nxt = hn_ref[i] * jnp.where(c < nc - 1, 1.0, 0.0)
        _to_segment_slabs(jnp.concatenate([hp_ref[i], hc_ref[i], nxt], axis=0), ext_scr.at[i],
                          LRU_SEG + 2 * HALO, FWD_PITCH)
        xl = _interleaved_rows(ext_scr.at[i], HALO - 2, LRU_SEG + 3, FWD_PITCH).astype(BF16)
        pe_dt = jnp.dot(xl[2 * SUBLANES:2 * SUBLANES + Q], wf_ref[:, XBC_DIM + LRU_WIDTH:],
                        preferred_element_type=F32)
        pe_l = jnp.dot(xl, wf_ref[:, XBC_DIM:XBC_DIM + LRU_WIDTH], preferred_element_type=F32)
        dk = _ssd_decay(pe_dt, dtb_ref[...], alog_ref[...], first)
        aux_o[i] = dk["ecs"]
        return xl, pe_l, dk

    def lru(i, xl, pe_l):
        pk = pk_o.at[i]
        live = jnp.logical_or(jnp.logical_not(first), _row_time(_row_iota(Q)) >= PAD)
        pe_s = []
        for j in range(LRU_TILES):
            cols = slice(j * LRU_TILE, (j + 1) * LRU_TILE)
            pe_s.append(jnp.dot(xl, wf_ref[:, 2 * j * LRU_TILE:2 * (j + 1) * LRU_TILE],
                                preferred_element_type=F32))
            xc = _dwconv4(pe_l[:, cols], lcw_ref[:, cols], lcb_ref[:, cols])
            pk[:, PK_XC + j * LRU_TILE:PK_XC + (j + 1) * LRU_TILE] = xc.astype(pk.dtype)
            _lru_tile(j, xc, live, wg_ref, brg_ref, big_ref, lam_ref, p_scr.at[i], hrow_scr.at[i],
                      h_scr.at[i], reverse=False)
        pk[:, PK_HF:] = hrow_scr[i].astype(pk.dtype)
        return jnp.concatenate(pe_s[:LRU_TILES // 2], axis=1), jnp.concatenate(pe_s[LRU_TILES // 2:], axis=1)

    def ssd(i, pe_x, pe_bc, dk):
        pk = pk_o.at[i]
        xs = _silu(_dwconv4(pe_x, scw_ref[:, :SSD_INNER], scb_ref[:, :SSD_INNER]))
        bc = _silu(_dwconv4(pe_bc, scw_ref[:, SSD_INNER:], scb_ref[:, SSD_INNER:]))
        pk[:, PK_XS:PK_CM] = xs.astype(pk.dtype)
        pk[:, PK_CM:PK_YP] = bc[:, BC_DIM:].astype(pk.dtype)
        y_part, st_b = _ssd_forward(xs, bc[:, :BC_DIM], bc[:, BC_DIM:], dk, sf_scr.at[i])
        pk[:, PK_YP:PK_XC] = y_part.astype(pk.dtype)
        sb_o[i] = st_b.astype(sb_o.dtype)

    xl, pe_l, dk = head(0)
    pe_x, pe_bc = lru(0, xl, pe_l)
    for i in range(1, SEQ_PER_STEP):
        xl, pe_l, dk_next = head(i)
        ssd(i - 1, pe_x, pe_bc, dk)
        pe_x, pe_bc = lru(i, xl, pe_l)
        dk = dk_next
    ssd(SEQ_PER_STEP - 1, pe_x, pe_bc, dk)


def _fwd(h, lw):
    bsz, lp, d = h.shape
    nc = lp // Q
    nh = lp // HALO
    per = Q // HALO
    w = LRU_WIDTH
    nb = SEQ_PER_STEP
    seq = lambda n: pl.BlockSpec((nb, Q, n), lambda i, c: (i, c, 0))
    return pl.pallas_call(
        _fwd_kernel,
        grid=(bsz // nb, nc),
        in_specs=[
            pl.BlockSpec((nb, HALO, d), lambda i, c: (i, jnp.maximum(c * per - 1, 0), 0)),
            seq(d),
            pl.BlockSpec((nb, HALO, d), lambda i, c: (i, jnp.minimum((c + 1) * per, nh - 1), 0)),
            _const_spec((d, XBC_DIM + w + LANES)),
            _const_spec((4, XBC_DIM)), _const_spec((1, XBC_DIM)), _const_spec((1, LANES)), _const_spec((1, LANES)),
            _const_spec((4, w)), _const_spec((1, w)), _const_spec((w // 256, 256, 512)),
            _const_spec((1, w)), _const_spec((1, w)), _const_spec((1, w)),
        ],
        out_specs=[
            seq(PK_WIDTH),
            pl.BlockSpec((nb, None, SSD_STATE, SSD_INNER), lambda i, c: (i, c, 0, 0)),
            seq(LANES),
        ],
        out_shape=[
            jax.ShapeDtypeStruct((bsz, lp, PK_WIDTH), ACT_DT),
            jax.ShapeDtypeStruct((bsz, nc, SSD_STATE, SSD_INNER), ACT_DT),
            jax.ShapeDtypeStruct((bsz, lp, LANES), F32),
        ],
        scratch_shapes=[pltpu.VMEM((nb, SSD_STATE, SSD_INNER), F32),
                        pltpu.VMEM((nb, d // LANES, SUBLANES * FWD_PITCH, LANES), F32),
                        pltpu.VMEM((nb, Q, w), F32), pltpu.VMEM((nb, Q, w), F32), pltpu.VMEM((nb, 1, w), F32)],
        compiler_params=_cparams("parallel", "arbitrary"),
        name="fwd_sweep",
    )(h, h, h, lw["w_fwd"], lw["ssd_conv_w"], lw["ssd_conv_b"], lw["dt_bias"], lw["a_log"],
      lw["lru_conv_w"], lw["lru_conv_b"], lw["wg"][0], lw["brg"][0], lw["big"][0], lw["lam"][0])


def _rev_kernel(h_ref, pk_ref, sb_ref, aux_ref,
                wr_ref, dskip_ref, ng_ref, wso_ref, wg_ref, brg_ref, big_ref, lam_ref, wlo_ref,
                wo_ref, g_ref, b_ref, o_ref,
                sb_scr, slab_scr, p_scr, hrow_scr, h_scr):
    @pl.when(pl.program_id(1) == 0)
    def _():
        sb_scr[...] = jnp.zeros_like(sb_scr)
        h_scr[...] = jnp.zeros_like(h_scr)

    k = pl.program_id(1)
    nc = pl.num_programs(1)

    def head(i):
        _to_segment_slabs(h_ref[i], slab_scr.at[i], LRU_SEG, REV_PITCH)
        return _interleaved_rows(slab_scr.at[i], 0, LRU_SEG, REV_PITCH)

    def body(i, h):
        pk = pk_ref.at[i]
        hb = h.astype(BF16)
        zl = []
        for j in range(LRU_TILES):
            zl.append(jnp.dot(hb, wr_ref[:, 2 * j * LRU_TILE:2 * (j + 1) * LRU_TILE],
                              preferred_element_type=F32))
            xc = pk[:, PK_XC + j * LRU_TILE:PK_XC + (j + 1) * LRU_TILE].astype(F32)
            _lru_tile(j, xc, None, wg_ref, brg_ref, big_ref, lam_ref, p_scr.at[i], hrow_scr.at[i],
                      h_scr.at[i], reverse=True)
        z = jnp.concatenate(zl[:LRU_TILES // 2], axis=1)
        lg = jnp.concatenate(zl[LRU_TILES // 2:], axis=1)

        y_off = _ssd_reverse(pk[:, PK_CM:PK_YP].astype(F32), aux_ref[i], sb_ref[i].astype(F32), sb_scr.at[i])
        y = pk[:, PK_YP:PK_XC].astype(F32) + y_off + pk[:, PK_XS:PK_CM].astype(F32) * dskip_ref[...]
        y = y * _silu(z)
        y = y * lax.rsqrt(jnp.mean(y * y, -1, keepdims=True) + RMS_EPS) * ng_ref[...]
        y_s = _dot(y, wso_ref[...])
        mg = jnp.dot(hb, wr_ref[:, SSD_INNER + LRU_WIDTH:], preferred_element_type=F32)
        y_l = _dot((pk[:, PK_HF:].astype(F32) + hrow_scr[i]) * _gelu(lg), wlo_ref[...])
        return _sigmoid(mg[:, :D_MODEL]) * y_s + _sigmoid(mg[:, D_MODEL:]) * y_l

    def tail(i, h, mix):
        out = _ln(ALPHA * h + _dot(mix, wo_ref[...]), g_ref[...], b_ref[...])
        out = jnp.where(jnp.logical_or(k < nc - 1, _row_time(_row_iota(Q)) >= PAD), out, 0.0)
        _to_slabs(out, slab_scr.at[i])
        per_seg = LRU_SEG // SUBLANES
        for r in range(Q // SUBLANES):
            start = (r % per_seg) * SUBLANES * SUBLANES + r // per_seg
            for s in range(slab_scr.shape[1]):
                o_ref[i, r * SUBLANES:(r + 1) * SUBLANES, s * LANES:(s + 1) * LANES] = (
                    slab_scr[i, s, pl.ds(start, SUBLANES, stride=SUBLANES), :])

    h = head(0)
    mix = body(0, h)
    for i in range(1, SEQ_PER_STEP):
        h_next = head(i)
        tail(i - 1, h, mix)
        h = h_next
        mix = body(i, h)
    tail(SEQ_PER_STEP - 1, h, mix)


def _rev(h, pk, sb, aux, lw):
    bsz, lp, d = h.shape
    nc = lp // Q
    w = LRU_WIDTH
    nb = SEQ_PER_STEP
    seq = lambda n: pl.BlockSpec((nb, Q, n), lambda i, c: (i, nc - 1 - c, 0))
    return pl.pallas_call(
        _rev_kernel,
        grid=(bsz // nb, nc),
        in_specs=[
            seq(d), seq(PK_WIDTH),
            pl.BlockSpec((nb, None, SSD_STATE, SSD_INNER), lambda i, c: (i, nc - 1 - c, 0, 0)),
            seq(LANES),
            _const_spec((d, SSD_INNER + w + 2 * d)),
            _const_spec((1, SSD_INNER)), _const_spec((1, SSD_INNER)), _const_spec((SSD_INNER, d)),
            _const_spec((w // 256, 256, 512)), _const_spec((1, w)), _const_spec((1, w)), _const_spec((1, w)),
            _const_spec((w, d)), _const_spec((d, d)), _const_spec((1, d)), _const_spec((1, d)),
        ],
        out_specs=seq(d),
        out_shape=jax.ShapeDtypeStruct((bsz, lp, d), F32),
        scratch_shapes=[pltpu.VMEM((nb, SSD_STATE, SSD_INNER), F32),
                        pltpu.VMEM((nb, d // LANES, SUBLANES * REV_PITCH, LANES), F32),
                        pltpu.VMEM((nb, Q, w), F32), pltpu.VMEM((nb, Q, w), F32), pltpu.VMEM((nb, 1, w), F32)],
        compiler_params=_cparams("parallel", "arbitrary"),
        name="rev_sweep",
    )(h, pk, sb, aux, lw["w_rev"], lw["dskip"], lw["norm_g"], lw["w_ssd_out"],
      lw["wg"][1], lw["brg"][1], lw["big"][1], lw["lam"][1], lw["w_lru_out"], lw["w_o"],
      lw["ln1_g"], lw["ln1_b"])


def _ffn_kernel(hp_ref, hc_ref, hn_ref, wu_ref, cw_ref, cb_ref, wd_ref, g_ref, b_ref, o_ref,
                ext_scr, out_scr):
    c = pl.program_id(1)
    nc = pl.num_programs(1)
    tf = hc_ref.shape[0]
    nseg = tf // SEG
    slabs = D_MODEL // LANES
    keep_next = jnp.where(c < nc - 1, 1.0, 0.0)
    for s in range(slabs):
        ls = slice(s * LANES, (s + 1) * LANES)
        ext_scr[s, 0:HALO, :] = hp_ref[:, ls]
        ext_scr[s, HALO:HALO + tf, :] = hc_ref[:, ls]
        ext_scr[s, HALO + tf:, :] = hn_ref[:, ls] * keep_next

    def group(g):
        halves = [jnp.concatenate([ext_scr[s, pl.ds(HALO + g + k * SUBLANES * nseg, SUBLANES, stride=nseg), :]
                                   for s in range(slabs)], axis=1) for k in range(SEG // SUBLANES)]
        return jnp.concatenate(halves, axis=0)
    xperm = jnp.concatenate([group(g) for g in range(-1, nseg + 1)], axis=0)
    xb = xperm.astype(BF16)

    def conv3(up, col):
        cw = cw_ref[:, pl.ds(col, FFN_COLS)].astype(BF16)
        acc = cb_ref[:, pl.ds(col, FFN_COLS)].astype(BF16) + up[0:tf] * cw[0:1, :]
        acc = acc + up[SEG:SEG + tf] * cw[1:2, :]
        return acc + up[2 * SEG:2 * SEG + tf] * cw[2:3, :]

    f = None
    for j in range(D_FF // FFN_COLS):
        cg, cv = j * FFN_COLS, D_FF + j * FFN_COLS
        ug = conv3(jnp.dot(xb, wu_ref[:, pl.ds(cg, FFN_COLS)], preferred_element_type=F32).astype(BF16), cg)
        uv = conv3(jnp.dot(xb, wu_ref[:, pl.ds(cv, FFN_COLS)], preferred_element_type=F32).astype(BF16), cv)
        part = jnp.dot(_gelu(ug) * uv, wd_ref[pl.ds(cg, FFN_COLS), :], preferred_element_type=F32)
        f = part if f is None else f + part
    y = _ln(ALPHA * xperm[SEG:SEG + tf] + f, g_ref[...], b_ref[...])
    p = _row_iota(tf)
    t = c * tf + (p & (SEG - 1)) * nseg + (p >> SEG_SHIFT)
    y = jnp.where(t >= PAD, y, 0.0)
    for j in range(nseg):
        for k in range(SEG // SUBLANES):
            r0 = j * SEG + k * SUBLANES
            for s in range(slabs):
                out_scr[s, pl.ds(j + k * SUBLANES * nseg, SUBLANES, stride=nseg), :] = (
                    y[r0:r0 + SUBLANES, s * LANES:(s + 1) * LANES])
    for s in range(slabs):
        o_ref[:, s * LANES:(s + 1) * LANES] = out_scr[s]


def _ffn_tile(lp):
    for parts in (4, 8, 2, 1):
        if lp % (parts * SEG) == 0 and lp // parts <= 640:
            return lp // parts
    return Q


def _ffn(h, lw):
    bsz, lp, d = h.shape
    tf = _ffn_tile(lp)
    nh = lp // HALO
    per = tf // HALO
    seq = pl.BlockSpec((None, tf, d), lambda i, c: (i, c, 0))
    return pl.pallas_call(
        _ffn_kernel,
        grid=(bsz, lp // tf),
        in_specs=[
            pl.BlockSpec((None, HALO, d), lambda i, c: (i, jnp.maximum(c * per - 1, 0), 0)),
            seq,
            pl.BlockSpec((None, HALO, d), lambda i, c: (i, jnp.minimum((c + 1) * per, nh - 1), 0)),
            _const_spec((d, 2 * D_FF)), _const_spec((3, 2 * D_FF)), _const_spec((1, 2 * D_FF)),
            _const_spec((D_FF, d)), _const_spec((1, d)), _const_spec((1, d)),
        ],
        out_specs=seq,
        out_shape=jax.ShapeDtypeStruct((bsz, lp, d), F32),
        scratch_shapes=[pltpu.VMEM((d // LANES, tf + 2 * HALO, LANES), F32),
                        pltpu.VMEM((d // LANES, tf, LANES), F32)],
        compiler_params=_cparams("parallel", "parallel"),
        name="ffn_ln",
    )(h, h, h, lw["w_up"], lw["ffn_conv_w"], lw["ffn_conv_b"], lw["w_down"], lw["ln2_g"], lw["ln2_b"])


def _pad_lanes(v):
    return jnp.pad(v.reshape(1, -1).astype(F32), ((0, 0), (0, LANES - v.size)))


def _gate_tiles(w_rg, w_ig):
    def tiles(w):
        w4 = w.reshape(LRU_HEADS // 4, 4, LRU_BLOCK, LRU_BLOCK)
        eye = jnp.eye(4, dtype=w.dtype)
        return jnp.einsum("thij,hk->thikj", w4, eye).reshape(LRU_HEADS // 4, 256, 256)
    return jnp.concatenate([tiles(w_rg), tiles(w_ig)], axis=-1).astype(BF16)


def _prep_layer(p, li):
    w_in = p["w_in"][li]
    row = lambda v: v.reshape(1, -1).astype(F32)
    w_dt = jnp.pad(w_in[:, S2:S3], ((0, 0), (0, LANES - (S3 - S2))))
    return dict(
        w_fwd=jnp.concatenate([w_in[:, S1:S2], w_in[:, S4:S5], w_dt], axis=1).astype(BF16),
        w_rev=jnp.concatenate([w_in[:, :S1], w_in[:, S3:S4], w_in[:, S5:]], axis=1).astype(BF16),
        ssd_conv_w=p["ssd_conv_w"][li], ssd_conv_b=row(p["ssd_conv_b"][li]),
        dt_bias=_pad_lanes(p["ssd_dt_bias"][li]), a_log=_pad_lanes(p["ssd_a_log"][li]),
        dskip=row(jnp.repeat(p["ssd_d"][li], SSD_HEAD_DIM)), norm_g=row(p["ssd_norm_g"][li]),
        w_ssd_out=p["w_ssd_out"][li].astype(BF16),
        lru_conv_w=p["lru_conv_w"][li], lru_conv_b=row(p["lru_conv_b"][li]),
        wg=[_gate_tiles(p["lru_w_rg"][li, d], p["lru_w_ig"][li, d]) for d in range(2)],
        brg=[row(p["lru_b_rg"][li, d]) for d in range(2)],
        big=[row(p["lru_b_ig"][li, d]) for d in range(2)],
        lam=[row(p["lru_lambda"][li, d]) for d in range(2)],
        w_lru_out=p["w_lru_out"][li].astype(BF16), w_o=p["w_o"][li].astype(BF16),
        ln1_g=row(p["ln1_g"][li]), ln1_b=row(p["ln1_b"][li]),
        w_up=p["w_up"][li].astype(BF16), ffn_conv_w=p["ffn_conv_w"][li],
        ffn_conv_b=row(p["ffn_conv_b"][li]), w_down=p["w_down"][li].astype(BF16),
        ln2_g=row(p["ln2_g"][li]), ln2_b=row(p["ln2_b"][li]),
    )


def _layer(h, lw):
    pk, sb, aux = _fwd(h, lw)
    h = _rev(h, pk, sb, aux, lw)
    return _ffn(h, lw)


def _run(x, meta_pad, ln_g, ln_b, layers):
    h = _embed(x, meta_pad, ln_g, ln_b)
    for lw in layers:
        h = _layer(h, lw)
    return h[:, Q:]


def kernel(x_prompt, x_sample, meta, ln_in_g, ln_in_b, w_in, ssd_conv_w, ssd_conv_b, ssd_dt_bias, ssd_a_log, ssd_d, ssd_norm_g, w_ssd_out, lru_conv_w, lru_conv_b, lru_w_rg, lru_b_rg, lru_w_ig, lru_b_ig, lru_lambda, w_lru_out, w_o, ln1_g, ln1_b, w_up, ffn_conv_w, ffn_conv_b, w_down, ln2_g, ln2_b):
    p = dict(w_in=w_in, ssd_conv_w=ssd_conv_w, ssd_conv_b=ssd_conv_b, ssd_dt_bias=ssd_dt_bias,
             ssd_a_log=ssd_a_log, ssd_d=ssd_d, ssd_norm_g=ssd_norm_g, w_ssd_out=w_ssd_out,
             lru_conv_w=lru_conv_w, lru_conv_b=lru_conv_b, lru_w_rg=lru_w_rg, lru_b_rg=lru_b_rg,
             lru_w_ig=lru_w_ig, lru_b_ig=lru_b_ig, lru_lambda=lru_lambda, w_lru_out=w_lru_out,
             w_o=w_o, ln1_g=ln1_g, ln1_b=ln1_b, w_up=w_up, ffn_conv_w=ffn_conv_w,
             ffn_conv_b=ffn_conv_b, w_down=w_down, ln2_g=ln2_g, ln2_b=ln2_b)
    layers = [_prep_layer(p, li) for li in range(w_in.shape[0])]
    meta_pad = jnp.pad(meta.astype(F32), ((PAD, 0), (0, 0)))
    g = ln_in_g.reshape(1, -1)
    b = ln_in_b.reshape(1, -1)
    return (_run(x_prompt, meta_pad, g, b, layers), _run(x_sample, meta_pad, g, b, layers))
```

```python
import math

import jax
import jax.numpy as jnp
from jax import lax
from jax.experimental import pallas as pl
from jax.experimental.pallas import tpu as pltpu

F32 = jnp.float32
BF16 = jnp.bfloat16

D_MODEL = 1024
DEPTH = 4
N_META = 16
SSD_HEADS = 16
SSD_HEAD_DIM = 64
SSD_INNER = SSD_HEADS * SSD_HEAD_DIM
SSD_GROUPS = 4
SSD_STATE = 128
BC_DIM = SSD_GROUPS * SSD_STATE
XBC_DIM = SSD_INNER + 2 * BC_DIM
LRU_WIDTH = 1024
LRU_HEADS = 16
LRU_BLOCK = LRU_WIDTH // LRU_HEADS
LRU_C = 8.0
D_FF = 3 * D_MODEL
S1 = SSD_INNER
S2 = S1 + XBC_DIM
S3 = S2 + 2 * SSD_HEADS
S4 = S3 + LRU_WIDTH
S5 = S4 + LRU_WIDTH
ALPHA = (2 * DEPTH) ** 0.25
LN_EPS = 1e-5
RMS_EPS = 1e-6

Q = 128
PAD = Q - N_META
HALO = 8
LANES = 128
SUBLANES = 8
LRU_SEG = Q // SUBLANES
FWD_PITCH = LRU_SEG + 2 * HALO + 8
REV_PITCH = LRU_SEG + 8
SEG_SHIFT = 4
SEG = 1 << SEG_SHIFT
FFN_COLS = 512
ACT_DT = BF16
SEQ_PER_STEP = 4
FWD_GROUP = 2
REV_GROUP = 1
VMEM_LIMIT = 56 * 1024 * 1024


def _cparams(*sem):
    return pltpu.CompilerParams(dimension_semantics=sem, vmem_limit_bytes=VMEM_LIMIT)


def _const_spec(shape):
    zeros = (0,) * len(shape)
    return pl.BlockSpec(shape, lambda i, c: zeros, pipeline_mode=pl.Buffered(1))


def _dot(a, b):
    return jnp.dot(a.astype(BF16), b.astype(BF16), preferred_element_type=F32)


def _dot_nt(a, b):
    return lax.dot_general(a.astype(BF16), b.astype(BF16), (((1,), (1,)), ((), ())),
                           preferred_element_type=F32)


def _split_dot(v, e, terms, e_left=False):
    acc = None
    r = v
    for _ in range(terms):
        hi = r.astype(BF16)
        part = (jnp.dot(e, hi, preferred_element_type=F32) if e_left
                else jnp.dot(hi, e, preferred_element_type=F32))
        acc = part if acc is None else acc + part
        r = r - hi.astype(F32)
    return acc


def _ln(x, g, b):
    mu = jnp.mean(x, -1, keepdims=True)
    xc = x - mu
    var = jnp.mean(xc * xc, -1, keepdims=True)
    return xc * lax.rsqrt(var + LN_EPS) * g + b


def _sigmoid(x):
    return 1.0 / (1.0 + jnp.exp(-x))


def _silu(x):
    h = 0.5 * x
    return h + h * jnp.tanh(h)


def _gelu(x):
    c = math.sqrt(2.0 / math.pi)
    return x * (0.5 * (1.0 + jnp.tanh(c * (x + 0.044715 * (x * x * x)))))


def _log1p(e):
    u = 1.0 + e
    return jnp.where(u == 1.0, e, jnp.log(u) * (e / (u - 1.0)))


def _softplus(x):
    return jnp.maximum(x, 0.0) + _log1p(jnp.exp(-jnp.abs(x)))


def _row_iota(n):
    return lax.broadcasted_iota(jnp.int32, (n, 1), 0)


def _row_time(p):
    return (p & (SUBLANES - 1)) * LRU_SEG + (p >> 3)


def _embed_kernel(x_ref, meta_ref, g_ref, b_ref, o_ref):
    c = pl.program_id(1)
    keep = jnp.logical_or(c > 0, _row_iota(Q) >= PAD)
    for i in range(x_ref.shape[0]):
        src = jnp.where(c == 0, meta_ref[...], x_ref[i])
        o_ref[i] = jnp.where(keep, _ln(src, g_ref[...], b_ref[...]), 0.0)


def _embed(x, meta_pad, g, b):
    bsz, t, d = x.shape
    nc = t // Q + 1
    nb = math.gcd(bsz, 8)
    return pl.pallas_call(
        _embed_kernel,
        grid=(bsz // nb, nc),
        in_specs=[
            pl.BlockSpec((nb, Q, d), lambda i, c: (i, jnp.maximum(c - 1, 0), 0)),
            pl.BlockSpec((Q, d), lambda i, c: (0, 0)),
            pl.BlockSpec((1, d), lambda i, c: (0, 0)),
            pl.BlockSpec((1, d), lambda i, c: (0, 0)),
        ],
        out_specs=pl.BlockSpec((nb, Q, d), lambda i, c: (i, c, 0)),
        out_shape=jax.ShapeDtypeStruct((bsz, nc * Q, d), F32),
        compiler_params=_cparams("parallel", "parallel"),
        name="embed_ln",
    )(x, meta_pad, g, b)


def _head_onehot(offset):
    r = lax.broadcasted_iota(jnp.int32, (LANES, SSD_INNER), 0)
    c = lax.broadcasted_iota(jnp.int32, (LANES, SSD_INNER), 1)
    return jnp.where(r - offset == c // SSD_HEAD_DIM, 1.0, 0.0).astype(BF16)


def _pair_rhs(x_pair):
    lane = lax.broadcasted_iota(jnp.int32, x_pair.shape, 1)
    lo = jnp.where(lane < SSD_HEAD_DIM, x_pair, 0.0)
    hi = jnp.where(lane >= SSD_HEAD_DIM, x_pair, 0.0)
    return jnp.concatenate([lo, hi], axis=0)


def _ssd_decay(dt_raw, dtb, alog, first):
    lane = lax.broadcasted_iota(jnp.int32, (1, LANES), 1)
    dt = _softplus(dt_raw + dtb)
    dt = jnp.where(jnp.logical_or(jnp.logical_not(first), _row_time(_row_iota(Q)) >= PAD), dt, 0.0)
    a = dt * (-jnp.exp(alog))
    ti = _row_time(lax.broadcasted_iota(jnp.int32, (Q, Q), 0))
    si = _row_time(lax.broadcasted_iota(jnp.int32, (Q, Q), 1))
    lower = jnp.where(si <= ti, 1.0, 0.0).astype(BF16)
    upper = jnp.where(si >= ti, 1.0, 0.0).astype(BF16)
    cs = jnp.where(lane < SSD_HEADS, _split_dot(a, lower, 3, True), _split_dot(a, upper, 3, True))
    edge = jnp.where(lane < SSD_HEADS, cs[Q - 1:Q, :], cs[0:1, :])
    wsrc = dt * jnp.exp(edge - cs)
    ecs = jnp.exp(cs)
    return dict(
        cs=cs, ecs=ecs, cs_t=cs.T, dt_t=dt.T, lt=si < ti, eq=si == ti,
        wf=_split_dot(wsrc, _head_onehot(0), 2), wb=_split_dot(wsrc, _head_onehot(SSD_HEADS), 2),
        dec=_split_dot(jnp.broadcast_to(ecs[Q - 1:Q, :], (8, LANES)), _head_onehot(0), 2)[0:1, :])


def _ssd_forward(xs, bm, cm, dk, sf_scr):
    cs, ecs, cs_t, dt_t, lt, eq = dk["cs"], dk["ecs"], dk["cs_t"], dk["dt_t"], dk["lt"], dk["eq"]
    sf = sf_scr[...]
    sf_b = sf.astype(BF16)
    xs_b = xs.astype(BF16)
    ys = []
    for j in range(SSD_HEADS // 2):
        g = (2 * j) // (SSD_HEADS // SSD_GROUPS)
        c_g = cm[:, g * SSD_STATE:(g + 1) * SSD_STATE]
        b_g = bm[:, g * SSD_STATE:(g + 1) * SSD_STATE]
        gmat = _dot_nt(c_g, b_g)
        lhs = []
        for h in (2 * j, 2 * j + 1):
            hb = SSD_HEADS + h
            e = jnp.where(lt, cs[:, h:h + 1] - cs_t[h:h + 1, :], cs[:, hb:hb + 1] - cs_t[hb:hb + 1, :])
            dsel = (jnp.where(lt, dt_t[h:h + 1, :], dt_t[hb:hb + 1, :])
                    + jnp.where(eq, dt_t[h:h + 1, :], 0.0))
            lhs.append((gmat * jnp.exp(e) * dsel).astype(BF16))
        for h in (2 * j, 2 * j + 1):
            lhs.append((c_g * ecs[:, h:h + 1]).astype(BF16))
        lhs = jnp.concatenate(lhs, axis=1)
        sl = slice(2 * j * SSD_HEAD_DIM, (2 * j + 2) * SSD_HEAD_DIM)
        rhs = jnp.concatenate([_pair_rhs(xs_b[:, sl]), _pair_rhs(sf_b[:, sl])], axis=0)
        ys.append(jnp.dot(lhs, rhs, preferred_element_type=F32))
    y_part = jnp.concatenate(ys, axis=1)

    xf = (xs * dk["wf"]).astype(BF16)
    xb = (xs * dk["wb"]).astype(BF16)
    st_f = []
    st_b = []
    for g in range(SSD_GROUPS):
        b_gt = bm[:, g * SSD_STATE:(g + 1) * SSD_STATE].T.astype(BF16)
        sl = slice(g * 256, (g + 1) * 256)
        st_f.append(jnp.dot(b_gt, xf[:, sl], preferred_element_type=F32))
        st_b.append(jnp.dot(b_gt, xb[:, sl], preferred_element_type=F32))
    sf_scr[...] = sf * dk["dec"] + jnp.concatenate(st_f, axis=1)
    return y_part, jnp.concatenate(st_b, axis=1)


def _ssd_reverse(cm, ecs, sb_chunk, sb_scr):
    sb = sb_scr[...]
    sb_b = sb.astype(BF16)
    ys = []
    for j in range(SSD_HEADS // 2):
        g = (2 * j) // (SSD_HEADS // SSD_GROUPS)
        c_g = cm[:, g * SSD_STATE:(g + 1) * SSD_STATE]
        lhs = [(c_g * ecs[:, SSD_HEADS + h:SSD_HEADS + h + 1]).astype(BF16) for h in (2 * j, 2 * j + 1)]
        sl = slice(2 * j * SSD_HEAD_DIM, (2 * j + 2) * SSD_HEAD_DIM)
        ys.append(jnp.dot(jnp.concatenate(lhs, axis=1), _pair_rhs(sb_b[:, sl]), preferred_element_type=F32))
    dec = _split_dot(jnp.broadcast_to(ecs[0:1, :], (8, LANES)), _head_onehot(SSD_HEADS), 2)[0:1, :]
    sb_scr[...] = sb * dec + sb_chunk
    return jnp.concatenate(ys, axis=1)


LRU_TILE = 4 * LRU_BLOCK
LRU_TILES = LRU_WIDTH // LRU_TILE


def _lru_gates(xc, wg, brg, big, lam):
    pre = jnp.dot(xc.astype(BF16), wg, preferred_element_type=F32)
    r = _sigmoid(pre[:, :LRU_TILE] + brg)
    i = _sigmoid(pre[:, LRU_TILE:] + big)
    log_a = (-LRU_C) * r * _softplus(-lam)
    a = jnp.exp(log_a)
    u = jnp.sqrt(-jnp.tanh(log_a) * (a * a + 1.0)) * (i * xc)
    return a, u


def _lru_scan(a, u, p_scr, o_scr, h_scr, cols, reverse):
    order = range(LRU_SEG - 1, -1, -1) if reverse else range(LRU_SEG)
    h = jnp.zeros((SUBLANES, a.shape[1]), F32)
    p = jnp.ones((SUBLANES, a.shape[1]), F32)
    for j in order:
        rows = slice(j * SUBLANES, (j + 1) * SUBLANES)
        h = a[rows] * h + u[rows]
        p = a[rows] * p
        o_scr[rows, cols] = h
        p_scr[rows, cols] = p
    carry = h_scr[:, cols]
    enter = [None] * SUBLANES
    for s in (range(SUBLANES - 1, -1, -1) if reverse else range(SUBLANES)):
        enter[s] = carry
        carry = h[s:s + 1, :] + p[s:s + 1, :] * carry
    h_scr[:, cols] = carry
    enter = jnp.concatenate(enter, axis=0)
    for j in order:
        rows = slice(j * SUBLANES, (j + 1) * SUBLANES)
        o_scr[rows, cols] = o_scr[rows, cols] + p_scr[rows, cols] * enter


def _lru_tile(j, xc, live, wg_ref, brg_ref, big_ref, lam_ref, p_scr, o_scr, h_scr, reverse):
    cols = slice(j * LRU_TILE, (j + 1) * LRU_TILE)
    a, u = _lru_gates(xc, wg_ref[j], brg_ref[:, cols], big_ref[:, cols], lam_ref[:, cols])
    if live is not None:
        u = jnp.where(live, u, 0.0)
    _lru_scan(a, u, p_scr, o_scr, h_scr, cols, reverse)


def _interleaved_rows(slab_scr, first, count, pitch):
    groups = [jnp.concatenate([slab_scr[s, pl.ds(first + g, SUBLANES, stride=pitch), :]
                               for s in range(slab_scr.shape[0])], axis=1) for g in range(count)]
    return jnp.concatenate(groups, axis=0)


def _to_segment_slabs(x, slab_scr, rows, pitch):
    for k in range(slab_scr.shape[0]):
        for s in range(SUBLANES):
            slab_scr[k, pitch * s:pitch * s + rows, :] = x[LRU_SEG * s:LRU_SEG * s + rows,
                                                           k * LANES:(k + 1) * LANES]


def _to_slabs(x, slab_scr):
    for s in range(slab_scr.shape[0]):
        slab_scr[s, 0:x.shape[0], :] = x[:, s * LANES:(s + 1) * LANES]


def _dwconv4(pe, cw, cb):
    acc = cb
    for k in range(4):
        acc = acc + pe[k * SUBLANES:k * SUBLANES + Q] * cw[k:k + 1, :]
    return acc


PK_XS, PK_CM, PK_YP, PK_XC, PK_HF = 0, SSD_INNER, SSD_INNER + BC_DIM, 2 * SSD_INNER + BC_DIM, 2 * SSD_INNER + BC_DIM + LRU_WIDTH
PK_WIDTH = PK_HF + LRU_WIDTH


def _fwd_kernel(hp_ref, hc_ref, hn_ref, wf_ref, scw_ref, scb_ref, dtb_ref, alog_ref,
                lcw_ref, lcb_ref, wg_ref, brg_ref, big_ref, lam_ref,
                pk_o, sb_o, aux_o,
                sf_scr, ext_scr, p_scr, hrow_scr, h_scr):
    c = pl.program_id(1)
    nc = pl.num_programs(1)
    first = c == 0

    @pl.when(first)
    def _():
        sf_scr[...] = jnp.zeros_like(sf_scr)
        h_scr[...] = jnp.zeros_like(h_scr)

    er = (LRU_SEG + 3) * SUBLANES

    def head(grp):
        xls = []
        for i in grp:
            nxt = hn_ref[i] * jnp.where(c < nc - 1, 1.0, 0.0)
            _to_segment_slabs(jnp.concatenate([hp_ref[i], hc_ref[i], nxt], axis=0), ext_scr.at[i],
                              LRU_SEG + 2 * HALO, FWD_PITCH)
            xls.append(_interleaved_rows(ext_scr.at[i], HALO - 2, LRU_SEG + 3, FWD_PITCH).astype(BF16))
        xl = jnp.concatenate(xls, axis=0)
        pe_dt = jnp.dot(jnp.concatenate([x[2 * SUBLANES:2 * SUBLANES + Q] for x in xls], axis=0),
                        wf_ref[:, XBC_DIM + LRU_WIDTH:], preferred_element_type=F32)
        pe_l = jnp.dot(xl, wf_ref[:, XBC_DIM:XBC_DIM + LRU_WIDTH], preferred_element_type=F32)
        dks = []
        for n, i in enumerate(grp):
            dks.append(_ssd_decay(pe_dt[n * Q:(n + 1) * Q], dtb_ref[...], alog_ref[...], first))
            aux_o[i] = dks[n]["ecs"]
        return xl, pe_l, dks

    def lru(grp, xl, pe_l):
        live = jnp.logical_or(jnp.logical_not(first), _row_time(_row_iota(Q)) >= PAD)
        pe_s = []
        for j in range(LRU_TILES):
            cols = slice(j * LRU_TILE, (j + 1) * LRU_TILE)
            pe_s.append(jnp.dot(xl, wf_ref[:, 2 * j * LRU_TILE:2 * (j + 1) * LRU_TILE],
                                preferred_element_type=F32))
            for n, i in enumerate(grp):
                xc = _dwconv4(pe_l[n * er:(n + 1) * er, cols], lcw_ref[:, cols], lcb_ref[:, cols])
                pk_o[i, :, PK_XC + j * LRU_TILE:PK_XC + (j + 1) * LRU_TILE] = xc.astype(pk_o.dtype)
                _lru_tile(j, xc, live, wg_ref, brg_ref, big_ref, lam_ref, p_scr.at[i], hrow_scr.at[i],
                          h_scr.at[i], reverse=False)
        for i in grp:
            pk_o[i, :, PK_HF:] = hrow_scr[i].astype(pk_o.dtype)
        return jnp.concatenate(pe_s[:LRU_TILES // 2], axis=1), jnp.concatenate(pe_s[LRU_TILES // 2:], axis=1)

    def ssd(grp, pe_x, pe_bc, dks):
        for n, i in enumerate(grp):
            pk = pk_o.at[i]
            rows = slice(n * er, (n + 1) * er)
            xs = _silu(_dwconv4(pe_x[rows], scw_ref[:, :SSD_INNER], scb_ref[:, :SSD_INNER]))
            bc = _silu(_dwconv4(pe_bc[rows], scw_ref[:, SSD_INNER:], scb_ref[:, SSD_INNER:]))
            pk[:, PK_XS:PK_CM] = xs.astype(pk.dtype)
            pk[:, PK_CM:PK_YP] = bc[:, BC_DIM:].astype(pk.dtype)
            y_part, st_b = _ssd_forward(xs, bc[:, :BC_DIM], bc[:, BC_DIM:], dks[n], sf_scr.at[i])
            pk[:, PK_YP:PK_XC] = y_part.astype(pk.dtype)
            sb_o[i] = st_b.astype(sb_o.dtype)

    groups = [tuple(range(g, g + FWD_GROUP)) for g in range(0, SEQ_PER_STEP, FWD_GROUP)]
    xl, pe_l, dks = head(groups[0])
    pe_x, pe_bc = lru(groups[0], xl, pe_l)
    for g in range(1, len(groups)):
        xl, pe_l, dks_next = head(groups[g])
        ssd(groups[g - 1], pe_x, pe_bc, dks)
        pe_x, pe_bc = lru(groups[g], xl, pe_l)
        dks = dks_next
    ssd(groups[-1], pe_x, pe_bc, dks)


def _fwd(h, lw):
    bsz, lp, d = h.shape
    nc = lp // Q
    nh = lp // HALO
    per = Q // HALO
    w = LRU_WIDTH
    nb = SEQ_PER_STEP
    seq = lambda n: pl.BlockSpec((nb, Q, n), lambda i, c: (i, c, 0))
    return pl.pallas_call(
        _fwd_kernel,
        grid=(bsz // nb, nc),
        in_specs=[
            pl.BlockSpec((nb, HALO, d), lambda i, c: (i, jnp.maximum(c * per - 1, 0), 0)),
            seq(d),
            pl.BlockSpec((nb, HALO, d), lambda i, c: (i, jnp.minimum((c + 1) * per, nh - 1), 0)),
            _const_spec((d, XBC_DIM + w + LANES)),
            _const_spec((4, XBC_DIM)), _const_spec((1, XBC_DIM)), _const_spec((1, LANES)), _const_spec((1, LANES)),
            _const_spec((4, w)), _const_spec((1, w)), _const_spec((w // 256, 256, 512)),
            _const_spec((1, w)), _const_spec((1, w)), _const_spec((1, w)),
        ],
        out_specs=[
            seq(PK_WIDTH),
            pl.BlockSpec((nb, None, SSD_STATE, SSD_INNER), lambda i, c: (i, c, 0, 0)),
            seq(LANES),
        ],
        out_shape=[
            jax.ShapeDtypeStruct((bsz, lp, PK_WIDTH), ACT_DT),
            jax.ShapeDtypeStruct((bsz, nc, SSD_STATE, SSD_INNER), ACT_DT),
            jax.ShapeDtypeStruct((bsz, lp, LANES), F32),
        ],
        scratch_shapes=[pltpu.VMEM((nb, SSD_STATE, SSD_INNER), F32),
                        pltpu.VMEM((nb, d // LANES, SUBLANES * FWD_PITCH, LANES), F32),
                        pltpu.VMEM((nb, Q, w), F32), pltpu.VMEM((nb, Q, w), F32), pltpu.VMEM((nb, 1, w), F32)],
        compiler_params=_cparams("parallel", "arbitrary"),
        name="fwd_sweep",
    )(h, h, h, lw["w_fwd"], lw["ssd_conv_w"], lw["ssd_conv_b"], lw["dt_bias"], lw["a_log"],
      lw["lru_conv_w"], lw["lru_conv_b"], lw["wg"][0], lw["brg"][0], lw["big"][0], lw["lam"][0])


def _rev_kernel(h_ref, pk_ref, sb_ref, aux_ref,
                wr_ref, dskip_ref, ng_ref, wso_ref, wg_ref, brg_ref, big_ref, lam_ref, wlo_ref,
                wo_ref, g_ref, b_ref, o_ref,
                sb_scr, slab_scr, p_scr, hrow_scr, h_scr):
    @pl.when(pl.program_id(1) == 0)
    def _():
        sb_scr[...] = jnp.zeros_like(sb_scr)
        h_scr[...] = jnp.zeros_like(h_scr)

    k = pl.program_id(1)
    nc = pl.num_programs(1)

    stack = lambda grp, f: jnp.concatenate([f(i) for i in grp], axis=0)

    def head(grp):
        def one(i):
            _to_segment_slabs(h_ref[i], slab_scr.at[i], LRU_SEG, REV_PITCH)
            return _interleaved_rows(slab_scr.at[i], 0, LRU_SEG, REV_PITCH)
        return stack(grp, one)

    def body(grp, h):
        hb = h.astype(BF16)
        zl = []
        for j in range(LRU_TILES):
            zl.append(jnp.dot(hb, wr_ref[:, 2 * j * LRU_TILE:2 * (j + 1) * LRU_TILE],
                              preferred_element_type=F32))
            for i in grp:
                xc = pk_ref[i, :, PK_XC + j * LRU_TILE:PK_XC + (j + 1) * LRU_TILE].astype(F32)
                _lru_tile(j, xc, None, wg_ref, brg_ref, big_ref, lam_ref, p_scr.at[i], hrow_scr.at[i],
                          h_scr.at[i], reverse=True)
        z = jnp.concatenate(zl[:LRU_TILES // 2], axis=1)
        lg = jnp.concatenate(zl[LRU_TILES // 2:], axis=1)

        y_off = stack(grp, lambda i: _ssd_reverse(pk_ref[i, :, PK_CM:PK_YP].astype(F32), aux_ref[i],
                                                  sb_ref[i].astype(F32), sb_scr.at[i]))
        y = (stack(grp, lambda i: pk_ref[i, :, PK_YP:PK_XC].astype(F32)) + y_off
             + stack(grp, lambda i: pk_ref[i, :, PK_XS:PK_CM].astype(F32)) * dskip_ref[...])
        y = y * _silu(z)
        y = y * lax.rsqrt(jnp.mean(y * y, -1, keepdims=True) + RMS_EPS) * ng_ref[...]
        y_s = _dot(y, wso_ref[...])
        mg = jnp.dot(hb, wr_ref[:, SSD_INNER + LRU_WIDTH:], preferred_element_type=F32)
        hsum = stack(grp, lambda i: pk_ref[i, :, PK_HF:].astype(F32) + hrow_scr[i])
        y_l = _dot(hsum * _gelu(lg), wlo_ref[...])
        return _sigmoid(mg[:, :D_MODEL]) * y_s + _sigmoid(mg[:, D_MODEL:]) * y_l

    def tail(grp, h, mix):
        out = _ln(ALPHA * h + _dot(mix, wo_ref[...]), g_ref[...], b_ref[...])
        live = jnp.logical_or(k < nc - 1, _row_time(_row_iota(Q)) >= PAD)
        per_seg = LRU_SEG // SUBLANES
        for n, i in enumerate(grp):
            _to_slabs(jnp.where(live, out[n * Q:(n + 1) * Q], 0.0), slab_scr.at[i])
            for r in range(Q // SUBLANES):
                start = (r % per_seg) * SUBLANES * SUBLANES + r // per_seg
                for s in range(slab_scr.shape[1]):
                    o_ref[i, r * SUBLANES:(r + 1) * SUBLANES, s * LANES:(s + 1) * LANES] = (
                        slab_scr[i, s, pl.ds(start, SUBLANES, stride=SUBLANES), :])

    groups = [tuple(range(g, g + REV_GROUP)) for g in range(0, SEQ_PER_STEP, REV_GROUP)]
    h = head(groups[0])
    mix = body(groups[0], h)
    for g in range(1, len(groups)):
        h_next = head(groups[g])
        tail(groups[g - 1], h, mix)
        h = h_next
        mix = body(groups[g], h)
    tail(groups[-1], h, mix)


def _rev(h, pk, sb, aux, lw):
    bsz, lp, d = h.shape
    nc = lp // Q
    w = LRU_WIDTH
    nb = SEQ_PER_STEP
    seq = lambda n: pl.BlockSpec((nb, Q, n), lambda i, c: (i, nc - 1 - c, 0))
    return pl.pallas_call(
        _rev_kernel,
        grid=(bsz // nb, nc),
        in_specs=[
            seq(d), seq(PK_WIDTH),
            pl.BlockSpec((nb, None, SSD_STATE, SSD_INNER), lambda i, c: (i, nc - 1 - c, 0, 0)),
            seq(LANES),
            _const_spec((d, SSD_INNER + w + 2 * d)),
            _const_spec((1, SSD_INNER)), _const_spec((1, SSD_INNER)), _const_spec((SSD_INNER, d)),
            _const_spec((w // 256, 256, 512)), _const_spec((1, w)), _const_spec((1, w)), _const_spec((1, w)),
            _const_spec((w, d)), _const_spec((d, d)), _const_spec((1, d)), _const_spec((1, d)),
        ],
        out_specs=seq(d),
        out_shape=jax.ShapeDtypeStruct((bsz, lp, d), F32),
        scratch_shapes=[pltpu.VMEM((nb, SSD_STATE, SSD_INNER), F32),
                        pltpu.VMEM((nb, d // LANES, SUBLANES * REV_PITCH, LANES), F32),
                        pltpu.VMEM((nb, Q, w), F32), pltpu.VMEM((nb, Q, w), F32), pltpu.VMEM((nb, 1, w), F32)],
        compiler_params=_cparams("parallel", "arbitrary"),
        name="rev_sweep",
    )(h, pk, sb, aux, lw["w_rev"], lw["dskip"], lw["norm_g"], lw["w_ssd_out"],
      lw["wg"][1], lw["brg"][1], lw["big"][1], lw["lam"][1], lw["w_lru_out"], lw["w_o"],
      lw["ln1_g"], lw["ln1_b"])


def _ffn_kernel(hp_ref, hc_ref, hn_ref, wu_ref, cw_ref, cb_ref, wd_ref, g_ref, b_ref, o_ref,
                ext_scr, out_scr):
    c = pl.program_id(1)
    nc = pl.num_programs(1)
    tf = hc_ref.shape[0]
    nseg = tf // SEG
    slabs = D_MODEL // LANES
    keep_next = jnp.where(c < nc - 1, 1.0, 0.0)
    for s in range(slabs):
        ls = slice(s * LANES, (s + 1) * LANES)
        ext_scr[s, 0:HALO, :] = hp_ref[:, ls]
        ext_scr[s, HALO:HALO + tf, :] = hc_ref[:, ls]
        ext_scr[s, HALO + tf:, :] = hn_ref[:, ls] * keep_next

    def group(g):
        halves = [jnp.concatenate([ext_scr[s, pl.ds(HALO + g + k * SUBLANES * nseg, SUBLANES, stride=nseg), :]
                                   for s in range(slabs)], axis=1) for k in range(SEG // SUBLANES)]
        return jnp.concatenate(halves, axis=0)
    xperm = jnp.concatenate([group(g) for g in range(-1, nseg + 1)], axis=0)
    xb = xperm.astype(BF16)

    def conv3(up, col):
        cw = cw_ref[:, pl.ds(col, FFN_COLS)].astype(BF16)
        acc = cb_ref[:, pl.ds(col, FFN_COLS)].astype(BF16) + up[0:tf] * cw[0:1, :]
        acc = acc + up[SEG:SEG + tf] * cw[1:2, :]
        return acc + up[2 * SEG:2 * SEG + tf] * cw[2:3, :]

    f = None
    for j in range(D_FF // FFN_COLS):
        cg, cv = j * FFN_COLS, D_FF + j * FFN_COLS
        ug = conv3(jnp.dot(xb, wu_ref[:, pl.ds(cg, FFN_COLS)], preferred_element_type=F32).astype(BF16), cg)
        uv = conv3(jnp.dot(xb, wu_ref[:, pl.ds(cv, FFN_COLS)], preferred_element_type=F32).astype(BF16), cv)
        part = jnp.dot(_gelu(ug) * uv, wd_ref[pl.ds(cg, FFN_COLS), :], preferred_element_type=F32)
        f = part if f is None else f + part
    y = _ln(ALPHA * xperm[SEG:SEG + tf] + f, g_ref[...], b_ref[...])
    p = _row_iota(tf)
    t = c * tf + (p & (SEG - 1)) * nseg + (p >> SEG_SHIFT)
    y = jnp.where(t >= PAD, y, 0.0)
    for j in range(nseg):
        for k in range(SEG // SUBLANES):
            r0 = j * SEG + k * SUBLANES
            for s in range(slabs):
                out_scr[s, pl.ds(j + k * SUBLANES * nseg, SUBLANES, stride=nseg), :] = (
                    y[r0:r0 + SUBLANES, s * LANES:(s + 1) * LANES])
    for s in range(slabs):
        o_ref[:, s * LANES:(s + 1) * LANES] = out_scr[s]


def _ffn_tile(lp):
    for parts in (4, 8, 2, 1):
        if lp % (parts * SEG) == 0 and lp // parts <= 640:
            return lp // parts
    return Q


def _ffn(h, lw):
    bsz, lp, d = h.shape
    tf = _ffn_tile(lp)
    nh = lp // HALO
    per = tf // HALO
    seq = pl.BlockSpec((None, tf, d), lambda i, c: (i, c, 0))
    return pl.pallas_call(
        _ffn_kernel,
        grid=(bsz, lp // tf),
        in_specs=[
            pl.BlockSpec((None, HALO, d), lambda i, c: (i, jnp.maximum(c * per - 1, 0), 0)),
            seq,
            pl.BlockSpec((None, HALO, d), lambda i, c: (i, jnp.minimum((c + 1) * per, nh - 1), 0)),
            _const_spec((d, 2 * D_FF)), _const_spec((3, 2 * D_FF)), _const_spec((1, 2 * D_FF)),
            _const_spec((D_FF, d)), _const_spec((1, d)), _const_spec((1, d)),
        ],
        out_specs=seq,
        out_shape=jax.ShapeDtypeStruct((bsz, lp, d), F32),
        scratch_shapes=[pltpu.VMEM((d // LANES, tf + 2 * HALO, LANES), F32),
                        pltpu.VMEM((d // LANES, tf, LANES), F32)],
        compiler_params=_cparams("parallel", "parallel"),
        name="ffn_ln",
    )(h, h, h, lw["w_up"], lw["ffn_conv_w"], lw["ffn_conv_b"], lw["w_down"], lw["ln2_g"], lw["ln2_b"])


def _pad_lanes(v):
    return jnp.pad(v.reshape(1, -1).astype(F32), ((0, 0), (0, LANES - v.size)))


def _gate_tiles(w_rg, w_ig):
    def tiles(w):
        w4 = w.reshape(LRU_HEADS // 4, 4, LRU_BLOCK, LRU_BLOCK)
        eye = jnp.eye(4, dtype=w.dtype)
        return jnp.einsum("thij,hk->thikj", w4, eye).reshape(LRU_HEADS // 4, 256, 256)
    return jnp.concatenate([tiles(w_rg), tiles(w_ig)], axis=-1).astype(BF16)


def _prep_layer(p, li):
    w_in = p["w_in"][li]
    row = lambda v: v.reshape(1, -1).astype(F32)
    w_dt = jnp.pad(w_in[:, S2:S3], ((0, 0), (0, LANES - (S3 - S2))))
    return dict(
        w_fwd=jnp.concatenate([w_in[:, S1:S2], w_in[:, S4:S5], w_dt], axis=1).astype(BF16),
        w_rev=jnp.concatenate([w_in[:, :S1], w_in[:, S3:S4], w_in[:, S5:]], axis=1).astype(BF16),
        ssd_conv_w=p["ssd_conv_w"][li], ssd_conv_b=row(p["ssd_conv_b"][li]),
        dt_bias=_pad_lanes(p["ssd_dt_bias"][li]), a_log=_pad_lanes(p["ssd_a_log"][li]),
        dskip=row(jnp.repeat(p["ssd_d"][li], SSD_HEAD_DIM)), norm_g=row(p["ssd_norm_g"][li]),
        w_ssd_out=p["w_ssd_out"][li].astype(BF16),
        lru_conv_w=p["lru_conv_w"][li], lru_conv_b=row(p["lru_conv_b"][li]),
        wg=[_gate_tiles(p["lru_w_rg"][li, d], p["lru_w_ig"][li, d]) for d in range(2)],
        brg=[row(p["lru_b_rg"][li, d]) for d in range(2)],
        big=[row(p["lru_b_ig"][li, d]) for d in range(2)],
        lam=[row(p["lru_lambda"][li, d]) for d in range(2)],
        w_lru_out=p["w_lru_out"][li].astype(BF16), w_o=p["w_o"][li].astype(BF16),
        ln1_g=row(p["ln1_g"][li]), ln1_b=row(p["ln1_b"][li]),
        w_up=p["w_up"][li].astype(BF16), ffn_conv_w=p["ffn_conv_w"][li],
        ffn_conv_b=row(p["ffn_conv_b"][li]), w_down=p["w_down"][li].astype(BF16),
        ln2_g=row(p["ln2_g"][li]), ln2_b=row(p["ln2_b"][li]),
    )


def _layer(h, lw):
    pk, sb, aux = _fwd(h, lw)
    h = _rev(h, pk, sb, aux, lw)
    return _ffn(h, lw)


def _run(x, meta_pad, ln_g, ln_b, layers):
    h = _embed(x, meta_pad, ln_g, ln_b)
    for lw in layers:
        h = _layer(h, lw)
    return h[:, Q:]


def kernel(x_prompt, x_sample, meta, ln_in_g, ln_in_b, w_in, ssd_conv_w, ssd_conv_b, ssd_dt_bias, ssd_a_log, ssd_d, ssd_norm_g, w_ssd_out, lru_conv_w, lru_conv_b, lru_w_rg, lru_b_rg, lru_w_ig, lru_b_ig, lru_lambda, w_lru_out, w_o, ln1_g, ln1_b, w_up, ffn_conv_w, ffn_conv_b, w_down, ln2_g, ln2_b):
    p = dict(w_in=w_in, ssd_conv_w=ssd_conv_w, ssd_conv_b=ssd_conv_b, ssd_dt_bias=ssd_dt_bias,
             ssd_a_log=ssd_a_log, ssd_d=ssd_d, ssd_norm_g=ssd_norm_g, w_ssd_out=w_ssd_out,
             lru_conv_w=lru_conv_w, lru_conv_b=lru_conv_b, lru_w_rg=lru_w_rg, lru_b_rg=lru_b_rg,
             lru_w_ig=lru_w_ig, lru_b_ig=lru_b_ig, lru_lambda=lru_lambda, w_lru_out=w_lru_out,
             w_o=w_o, ln1_g=ln1_g, ln1_b=ln1_b, w_up=w_up, ffn_conv_w=ffn_conv_w,
             ffn_conv_b=ffn_conv_b, w_down=w_down, ln2_g=ln2_g, ln2_b=ln2_b)
    layers = [_prep_layer(p, li) for li in range(w_in.shape[0])]
    meta_pad = jnp.pad(meta.astype(F32), ((PAD, 0), (0, 0)))
    g = ln_in_g.reshape(1, -1)
    b = ln_in_b.reshape(1, -1)
    return (_run(x_prompt, meta_pad, g, b, layers), _run(x_sample, meta_pad, g, b, layers))
```

```python
import math

import jax
import jax.numpy as jnp
from jax import lax
from jax.experimental import pallas as pl
from jax.experimental.pallas import tpu as pltpu

F32 = jnp.float32
BF16 = jnp.bfloat16

D_MODEL = 1024
DEPTH = 4
N_META = 16
SSD_HEADS = 16
SSD_HEAD_DIM = 64
SSD_INNER = SSD_HEADS * SSD_HEAD_DIM
SSD_GROUPS = 4
SSD_STATE = 128
BC_DIM = SSD_GROUPS * SSD_STATE
XBC_DIM = SSD_INNER + 2 * BC_DIM
LRU_WIDTH = 1024
LRU_HEADS = 16
LRU_BLOCK = LRU_WIDTH // LRU_HEADS
LRU_C = 8.0
D_FF = 3 * D_MODEL
S1 = SSD_INNER
S2 = S1 + XBC_DIM
S3 = S2 + 2 * SSD_HEADS
S4 = S3 + LRU_WIDTH
S5 = S4 + LRU_WIDTH
ALPHA = (2 * DEPTH) ** 0.25
LN_EPS = 1e-5
RMS_EPS = 1e-6

Q = 128
PAD = Q - N_META
HALO = 8
LANES = 128
SUBLANE_SHIFT = 3
SUBLANES = 1 << SUBLANE_SHIFT
LRU_SEG = Q // SUBLANES
FWD_PITCH = LRU_SEG + 2 * HALO + 8
REV_PITCH = LRU_SEG + 8
SEG_SHIFT = 4
SEG = 1 << SEG_SHIFT
FFN_COLS = 512
ACT_DT = BF16
SEQ_PER_STEP = 4
FWD_GROUP = 2
REV_GROUP = 1
VMEM_LIMIT = 56 * 1024 * 1024


def _cparams(*sem):
    return pltpu.CompilerParams(dimension_semantics=sem, vmem_limit_bytes=VMEM_LIMIT)


def _const_spec(shape):
    zeros = (0,) * len(shape)
    return pl.BlockSpec(shape, lambda i, c: zeros, pipeline_mode=pl.Buffered(1))


def _dot(a, b):
    return jnp.dot(a.astype(BF16), b.astype(BF16), preferred_element_type=F32)


def _dot_nt(a, b):
    return lax.dot_general(a.astype(BF16), b.astype(BF16), (((1,), (1,)), ((), ())),
                           preferred_element_type=F32)


def _split_dot(v, e, terms, e_left=False):
    acc = None
    r = v
    for _ in range(terms):
        hi = r.astype(BF16)
        part = (jnp.dot(e, hi, preferred_element_type=F32) if e_left
                else jnp.dot(hi, e, preferred_element_type=F32))
        acc = part if acc is None else acc + part
        r = r - hi.astype(F32)
    return acc


def _ln(x, g, b):
    mu = jnp.mean(x, -1, keepdims=True)
    xc = x - mu
    var = jnp.mean(xc * xc, -1, keepdims=True)
    return xc * lax.rsqrt(var + LN_EPS) * g + b


def _sigmoid(x):
    return 1.0 / (1.0 + jnp.exp(-x))


def _silu(x):
    h = 0.5 * x
    return h + h * jnp.tanh(h)


def _gelu(x):
    c = math.sqrt(2.0 / math.pi)
    return x * (0.5 * (1.0 + jnp.tanh(c * (x + 0.044715 * (x * x * x)))))


def _log1p(e):
    u = 1.0 + e
    return jnp.where(u == 1.0, e, jnp.log(u) * (e / (u - 1.0)))


def _softplus(x):
    return jnp.maximum(x, 0.0) + _log1p(jnp.exp(-jnp.abs(x)))


def _row_iota(n):
    return lax.broadcasted_iota(jnp.int32, (n, 1), 0)


def _row_time(p):
    return (p & (SUBLANES - 1)) * LRU_SEG + (p >> SUBLANE_SHIFT)


def _embed_kernel(x_ref, meta_ref, g_ref, b_ref, o_ref):
    c = pl.program_id(1)
    keep = jnp.logical_or(c > 0, _row_iota(Q) >= PAD)
    for i in range(x_ref.shape[0]):
        src = jnp.where(c == 0, meta_ref[...], x_ref[i])
        o_ref[i] = jnp.where(keep, _ln(src, g_ref[...], b_ref[...]), 0.0)


def _embed(x, meta_pad, g, b):
    bsz, t, d = x.shape
    nc = t // Q + 1
    nb = math.gcd(bsz, 8)
    return pl.pallas_call(
        _embed_kernel,
        grid=(bsz // nb, nc),
        in_specs=[
            pl.BlockSpec((nb, Q, d), lambda i, c: (i, jnp.maximum(c - 1, 0), 0)),
            pl.BlockSpec((Q, d), lambda i, c: (0, 0)),
            pl.BlockSpec((1, d), lambda i, c: (0, 0)),
            pl.BlockSpec((1, d), lambda i, c: (0, 0)),
        ],
        out_specs=pl.BlockSpec((nb, Q, d), lambda i, c: (i, c, 0)),
        out_shape=jax.ShapeDtypeStruct((bsz, nc * Q, d), F32),
        compiler_params=_cparams("parallel", "parallel"),
        name="embed_ln",
    )(x, meta_pad, g, b)


def _head_onehot(offset):
    r = lax.broadcasted_iota(jnp.int32, (LANES, SSD_INNER), 0)
    c = lax.broadcasted_iota(jnp.int32, (LANES, SSD_INNER), 1)
    return jnp.where(r - offset == c // SSD_HEAD_DIM, 1.0, 0.0).astype(BF16)


def _pair_rhs(x_pair):
    lane = lax.broadcasted_iota(jnp.int32, x_pair.shape, 1)
    lo = jnp.where(lane < SSD_HEAD_DIM, x_pair, 0.0)
    hi = jnp.where(lane >= SSD_HEAD_DIM, x_pair, 0.0)
    return jnp.concatenate([lo, hi], axis=0)


def _ssd_decay(dt_raw, dtb, alog, first):
    lane = lax.broadcasted_iota(jnp.int32, (1, LANES), 1)
    dt = _softplus(dt_raw + dtb)
    dt = jnp.where(jnp.logical_or(jnp.logical_not(first), _row_time(_row_iota(Q)) >= PAD), dt, 0.0)
    a = dt * (-jnp.exp(alog))
    ti = _row_time(lax.broadcasted_iota(jnp.int32, (Q, Q), 0))
    si = _row_time(lax.broadcasted_iota(jnp.int32, (Q, Q), 1))
    lower = jnp.where(si <= ti, 1.0, 0.0).astype(BF16)
    upper = jnp.where(si >= ti, 1.0, 0.0).astype(BF16)
    cs = jnp.where(lane < SSD_HEADS, _split_dot(a, lower, 3, True), _split_dot(a, upper, 3, True))
    edge = jnp.where(lane < SSD_HEADS, cs[Q - 1:Q, :], cs[0:1, :])
    wsrc = dt * jnp.exp(edge - cs)
    ecs = jnp.exp(cs)
    return dict(
        cs=cs, ecs=ecs, cs_t=cs.T, dt_t=dt.T, lt=si < ti, eq=si == ti,
        wf=_split_dot(wsrc, _head_onehot(0), 2), wb=_split_dot(wsrc, _head_onehot(SSD_HEADS), 2),
        dec=_split_dot(jnp.broadcast_to(ecs[Q - 1:Q, :], (8, LANES)), _head_onehot(0), 2)[0:1, :])


def _ssd_forward(xs, bm, cm, dk, sf_scr):
    cs, ecs, cs_t, dt_t, lt, eq = dk["cs"], dk["ecs"], dk["cs_t"], dk["dt_t"], dk["lt"], dk["eq"]
    sf = sf_scr[...]
    sf_b = sf.astype(BF16)
    xs_b = xs.astype(BF16)
    ys = []
    for j in range(SSD_HEADS // 2):
        g = (2 * j) // (SSD_HEADS // SSD_GROUPS)
        c_g = cm[:, g * SSD_STATE:(g + 1) * SSD_STATE]
        b_g = bm[:, g * SSD_STATE:(g + 1) * SSD_STATE]
        gmat = _dot_nt(c_g, b_g)
        lhs = []
        for h in (2 * j, 2 * j + 1):
            hb = SSD_HEADS + h
            e = jnp.where(lt, cs[:, h:h + 1] - cs_t[h:h + 1, :], cs[:, hb:hb + 1] - cs_t[hb:hb + 1, :])
            dsel = (jnp.where(lt, dt_t[h:h + 1, :], dt_t[hb:hb + 1, :])
                    + jnp.where(eq, dt_t[h:h + 1, :], 0.0))
            lhs.append((gmat * jnp.exp(e) * dsel).astype(BF16))
        for h in (2 * j, 2 * j + 1):
            lhs.append((c_g * ecs[:, h:h + 1]).astype(BF16))
        lhs = jnp.concatenate(lhs, axis=1)
        sl = slice(2 * j * SSD_HEAD_DIM, (2 * j + 2) * SSD_HEAD_DIM)
        rhs = jnp.concatenate([_pair_rhs(xs_b[:, sl]), _pair_rhs(sf_b[:, sl])], axis=0)
        ys.append(jnp.dot(lhs, rhs, preferred_element_type=F32))
    y_part = jnp.concatenate(ys, axis=1)

    xf = (xs * dk["wf"]).astype(BF16)
    xb = (xs * dk["wb"]).astype(BF16)
    st_f = []
    st_b = []
    for g in range(SSD_GROUPS):
        b_gt = bm[:, g * SSD_STATE:(g + 1) * SSD_STATE].T.astype(BF16)
        sl = slice(g * 256, (g + 1) * 256)
        st_f.append(jnp.dot(b_gt, xf[:, sl], preferred_element_type=F32))
        st_b.append(jnp.dot(b_gt, xb[:, sl], preferred_element_type=F32))
    sf_scr[...] = sf * dk["dec"] + jnp.concatenate(st_f, axis=1)
    return y_part, jnp.concatenate(st_b, axis=1)


def _ssd_reverse(cm, ecs, sb_chunk, sb_scr):
    sb = sb_scr[...]
    sb_b = sb.astype(BF16)
    ys = []
    for j in range(SSD_HEADS // 2):
        g = (2 * j) // (SSD_HEADS // SSD_GROUPS)
        c_g = cm[:, g * SSD_STATE:(g + 1) * SSD_STATE]
        lhs = [(c_g * ecs[:, SSD_HEADS + h:SSD_HEADS + h + 1]).astype(BF16) for h in (2 * j, 2 * j + 1)]
        sl = slice(2 * j * SSD_HEAD_DIM, (2 * j + 2) * SSD_HEAD_DIM)
        ys.append(jnp.dot(jnp.concatenate(lhs, axis=1), _pair_rhs(sb_b[:, sl]), preferred_element_type=F32))
    dec = _split_dot(jnp.broadcast_to(ecs[0:1, :], (8, LANES)), _head_onehot(SSD_HEADS), 2)[0:1, :]
    sb_scr[...] = sb * dec + sb_chunk
    return jnp.concatenate(ys, axis=1)


LRU_TILE = 4 * LRU_BLOCK
LRU_TILES = LRU_WIDTH // LRU_TILE


def _lru_gates(xc, wg, brg, big, lam):
    pre = jnp.dot(xc.astype(BF16), wg, preferred_element_type=F32)
    r = _sigmoid(pre[:, :LRU_TILE] + brg)
    i = _sigmoid(pre[:, LRU_TILE:] + big)
    log_a = (-LRU_C) * r * _softplus(-lam)
    a = jnp.exp(log_a)
    u = jnp.sqrt(-jnp.tanh(log_a) * (a * a + 1.0)) * (i * xc)
    return a, u


def _lru_scan(a, u, p_scr, o_scr, h_scr, cols, reverse):
    order = range(LRU_SEG - 1, -1, -1) if reverse else range(LRU_SEG)
    h = jnp.zeros((SUBLANES, a.shape[1]), F32)
    p = jnp.ones((SUBLANES, a.shape[1]), F32)
    for j in order:
        rows = slice(j * SUBLANES, (j + 1) * SUBLANES)
        h = a[rows] * h + u[rows]
        p = a[rows] * p
        o_scr[rows, cols] = h
        p_scr[rows, cols] = p
    carry = h_scr[:, cols]
    enter = [None] * SUBLANES
    for s in (range(SUBLANES - 1, -1, -1) if reverse else range(SUBLANES)):
        enter[s] = carry
        carry = h[s:s + 1, :] + p[s:s + 1, :] * carry
    h_scr[:, cols] = carry
    enter = jnp.concatenate(enter, axis=0)
    for j in order:
        rows = slice(j * SUBLANES, (j + 1) * SUBLANES)
        o_scr[rows, cols] = o_scr[rows, cols] + p_scr[rows, cols] * enter


def _lru_tile(j, xc, live, wg_ref, brg_ref, big_ref, lam_ref, p_scr, o_scr, h_scr, reverse):
    cols = slice(j * LRU_TILE, (j + 1) * LRU_TILE)
    a, u = _lru_gates(xc, wg_ref[j], brg_ref[:, cols], big_ref[:, cols], lam_ref[:, cols])
    if live is not None:
        u = jnp.where(live, u, 0.0)
    _lru_scan(a, u, p_scr, o_scr, h_scr, cols, reverse)


def _interleaved_rows(slab_scr, first, count, pitch):
    groups = [jnp.concatenate([slab_scr[s, pl.ds(first + g, SUBLANES, stride=pitch), :]
                               for s in range(slab_scr.shape[0])], axis=1) for g in range(count)]
    return jnp.concatenate(groups, axis=0)


def _to_segment_slabs(x, slab_scr, rows, pitch):
    for k in range(slab_scr.shape[0]):
        for s in range(SUBLANES):
            slab_scr[k, pitch * s:pitch * s + rows, :] = x[LRU_SEG * s:LRU_SEG * s + rows,
                                                           k * LANES:(k + 1) * LANES]


def _to_slabs(x, slab_scr):
    for s in range(slab_scr.shape[0]):
        slab_scr[s, 0:x.shape[0], :] = x[:, s * LANES:(s + 1) * LANES]


def _dwconv4(pe, cw, cb):
    acc = cb
    for k in range(4):
        acc = acc + pe[k * SUBLANES:k * SUBLANES + Q] * cw[k:k + 1, :]
    return acc


PK_XS = 0
PK_CM = PK_XS + SSD_INNER
PK_YP = PK_CM + BC_DIM
PK_XC = PK_YP + SSD_INNER
PK_HF = PK_XC + LRU_WIDTH
PK_WIDTH = PK_HF + LRU_WIDTH


def _fwd_kernel(hp_ref, hc_ref, hn_ref, wf_ref, scw_ref, scb_ref, dtb_ref, alog_ref,
                lcw_ref, lcb_ref, wg_ref, brg_ref, big_ref, lam_ref,
                pk_o, sb_o, aux_o,
                sf_scr, ext_scr, p_scr, hrow_scr, h_scr):
    c = pl.program_id(1)
    nc = pl.num_programs(1)
    first = c == 0

    @pl.when(first)
    def _():
        sf_scr[...] = jnp.zeros_like(sf_scr)
        h_scr[...] = jnp.zeros_like(h_scr)

    er = (LRU_SEG + 3) * SUBLANES

    def head(grp):
        xls = []
        for i in grp:
            nxt = hn_ref[i] * jnp.where(c < nc - 1, 1.0, 0.0)
            _to_segment_slabs(jnp.concatenate([hp_ref[i], hc_ref[i], nxt], axis=0), ext_scr.at[i],
                              LRU_SEG + 2 * HALO, FWD_PITCH)
            xls.append(_interleaved_rows(ext_scr.at[i], HALO - 2, LRU_SEG + 3, FWD_PITCH).astype(BF16))
        xl = jnp.concatenate(xls, axis=0)
        pe_dt = jnp.dot(jnp.concatenate([x[2 * SUBLANES:2 * SUBLANES + Q] for x in xls], axis=0),
                        wf_ref[:, XBC_DIM + LRU_WIDTH:], preferred_element_type=F32)
        pe_l = jnp.dot(xl, wf_ref[:, XBC_DIM:XBC_DIM + LRU_WIDTH], preferred_element_type=F32)
        dks = []
        for n, i in enumerate(grp):
            dks.append(_ssd_decay(pe_dt[n * Q:(n + 1) * Q], dtb_ref[...], alog_ref[...], first))
            aux_o[i] = dks[n]["ecs"]
        return xl, pe_l, dks

    def lru(grp, xl, pe_l):
        live = jnp.logical_or(jnp.logical_not(first), _row_time(_row_iota(Q)) >= PAD)
        pe_s = []
        for j in range(LRU_TILES):
            cols = slice(j * LRU_TILE, (j + 1) * LRU_TILE)
            pe_s.append(jnp.dot(xl, wf_ref[:, 2 * j * LRU_TILE:2 * (j + 1) * LRU_TILE],
                                preferred_element_type=F32))
            for n, i in enumerate(grp):
                xc = _dwconv4(pe_l[n * er:(n + 1) * er, cols], lcw_ref[:, cols], lcb_ref[:, cols])
                pk_o[i, :, PK_XC + j * LRU_TILE:PK_XC + (j + 1) * LRU_TILE] = xc.astype(pk_o.dtype)
                _lru_tile(j, xc, live, wg_ref, brg_ref, big_ref, lam_ref, p_scr.at[i], hrow_scr.at[i],
                          h_scr.at[i], reverse=False)
        for i in grp:
            pk_o[i, :, PK_HF:] = hrow_scr[i].astype(pk_o.dtype)
        return jnp.concatenate(pe_s[:LRU_TILES // 2], axis=1), jnp.concatenate(pe_s[LRU_TILES // 2:], axis=1)

    def ssd(grp, pe_x, pe_bc, dks):
        for n, i in enumerate(grp):
            pk = pk_o.at[i]
            rows = slice(n * er, (n + 1) * er)
            xs = _silu(_dwconv4(pe_x[rows], scw_ref[:, :SSD_INNER], scb_ref[:, :SSD_INNER]))
            bc = _silu(_dwconv4(pe_bc[rows], scw_ref[:, SSD_INNER:], scb_ref[:, SSD_INNER:]))
            pk[:, PK_XS:PK_CM] = xs.astype(pk.dtype)
            pk[:, PK_CM:PK_YP] = bc[:, BC_DIM:].astype(pk.dtype)
            y_part, st_b = _ssd_forward(xs, bc[:, :BC_DIM], bc[:, BC_DIM:], dks[n], sf_scr.at[i])
            pk[:, PK_YP:PK_XC] = y_part.astype(pk.dtype)
            sb_o[i] = st_b.astype(sb_o.dtype)

    groups = [tuple(range(g, g + FWD_GROUP)) for g in range(0, SEQ_PER_STEP, FWD_GROUP)]
    xl, pe_l, dks = head(groups[0])
    pe_x, pe_bc = lru(groups[0], xl, pe_l)
    for g in range(1, len(groups)):
        xl, pe_l, dks_next = head(groups[g])
        ssd(groups[g - 1], pe_x, pe_bc, dks)
        pe_x, pe_bc = lru(groups[g], xl, pe_l)
        dks = dks_next
    ssd(groups[-1], pe_x, pe_bc, dks)


def _fwd(h, lw):
    bsz, lp, d = h.shape
    nc = lp // Q
    nh = lp // HALO
    per = Q // HALO
    w = LRU_WIDTH
    nb = SEQ_PER_STEP
    seq = lambda n: pl.BlockSpec((nb, Q, n), lambda i, c: (i, c, 0))
    return pl.pallas_call(
        _fwd_kernel,
        grid=(bsz // nb, nc),
        in_specs=[
            pl.BlockSpec((nb, HALO, d), lambda i, c: (i, jnp.maximum(c * per - 1, 0), 0)),
            seq(d),
            pl.BlockSpec((nb, HALO, d), lambda i, c: (i, jnp.minimum((c + 1) * per, nh - 1), 0)),
            _const_spec((d, XBC_DIM + w + LANES)),
            _const_spec((4, XBC_DIM)), _const_spec((1, XBC_DIM)), _const_spec((1, LANES)), _const_spec((1, LANES)),
            _const_spec((4, w)), _const_spec((1, w)), _const_spec((w // 256, 256, 512)),
            _const_spec((1, w)), _const_spec((1, w)), _const_spec((1, w)),
        ],
        out_specs=[
            seq(PK_WIDTH),
            pl.BlockSpec((nb, None, SSD_STATE, SSD_INNER), lambda i, c: (i, c, 0, 0)),
            seq(LANES),
        ],
        out_shape=[
            jax.ShapeDtypeStruct((bsz, lp, PK_WIDTH), ACT_DT),
            jax.ShapeDtypeStruct((bsz, nc, SSD_STATE, SSD_INNER), ACT_DT),
            jax.ShapeDtypeStruct((bsz, lp, LANES), F32),
        ],
        scratch_shapes=[pltpu.VMEM((nb, SSD_STATE, SSD_INNER), F32),
                        pltpu.VMEM((nb, d // LANES, SUBLANES * FWD_PITCH, LANES), F32),
                        pltpu.VMEM((nb, Q, w), F32), pltpu.VMEM((nb, Q, w), F32), pltpu.VMEM((nb, 1, w), F32)],
        compiler_params=_cparams("parallel", "arbitrary"),
        name="fwd_sweep",
    )(h, h, h, lw["w_fwd"], lw["ssd_conv_w"], lw["ssd_conv_b"], lw["dt_bias"], lw["a_log"],
      lw["lru_conv_w"], lw["lru_conv_b"], lw["wg"][0], lw["brg"][0], lw["big"][0], lw["lam"][0])


def _rev_kernel(h_ref, pk_ref, sb_ref, aux_ref,
                wr_ref, dskip_ref, ng_ref, wso_ref, wg_ref, brg_ref, big_ref, lam_ref, wlo_ref,
                wo_ref, g_ref, b_ref, o_ref,
                sb_scr, slab_scr, p_scr, hrow_scr, h_scr):
    k = pl.program_id(1)
    nc = pl.num_programs(1)

    @pl.when(k == 0)
    def _():
        sb_scr[...] = jnp.zeros_like(sb_scr)
        h_scr[...] = jnp.zeros_like(h_scr)

    stack = lambda grp, f: jnp.concatenate([f(i) for i in grp], axis=0)

    def head(grp):
        def one(i):
            _to_segment_slabs(h_ref[i], slab_scr.at[i], LRU_SEG, REV_PITCH)
            return _interleaved_rows(slab_scr.at[i], 0, LRU_SEG, REV_PITCH)
        return stack(grp, one)

    def body(grp, h):
        hb = h.astype(BF16)
        zl = []
        for j in range(LRU_TILES):
            zl.append(jnp.dot(hb, wr_ref[:, 2 * j * LRU_TILE:2 * (j + 1) * LRU_TILE],
                              preferred_element_type=F32))
            for i in grp:
                xc = pk_ref[i, :, PK_XC + j * LRU_TILE:PK_XC + (j + 1) * LRU_TILE].astype(F32)
                _lru_tile(j, xc, None, wg_ref, brg_ref, big_ref, lam_ref, p_scr.at[i], hrow_scr.at[i],
                          h_scr.at[i], reverse=True)
        z = jnp.concatenate(zl[:LRU_TILES // 2], axis=1)
        lg = jnp.concatenate(zl[LRU_TILES // 2:], axis=1)

        y_off = stack(grp, lambda i: _ssd_reverse(pk_ref[i, :, PK_CM:PK_YP].astype(F32), aux_ref[i],
                                                  sb_ref[i].astype(F32), sb_scr.at[i]))
        y = (stack(grp, lambda i: pk_ref[i, :, PK_YP:PK_XC].astype(F32)) + y_off
             + stack(grp, lambda i: pk_ref[i, :, PK_XS:PK_CM].astype(F32)) * dskip_ref[...])
        y = y * _silu(z)
        y = y * lax.rsqrt(jnp.mean(y * y, -1, keepdims=True) + RMS_EPS) * ng_ref[...]
        y_s = _dot(y, wso_ref[...])
        mg = jnp.dot(hb, wr_ref[:, SSD_INNER + LRU_WIDTH:], preferred_element_type=F32)
        hsum = stack(grp, lambda i: pk_ref[i, :, PK_HF:].astype(F32) + hrow_scr[i])
        y_l = _dot(hsum * _gelu(lg), wlo_ref[...])
        return _sigmoid(mg[:, :D_MODEL]) * y_s + _sigmoid(mg[:, D_MODEL:]) * y_l

    def tail(grp, h, mix):
        out = _ln(ALPHA * h + _dot(mix, wo_ref[...]), g_ref[...], b_ref[...])
        live = jnp.logical_or(k < nc - 1, _row_time(_row_iota(Q)) >= PAD)
        per_seg = LRU_SEG // SUBLANES
        for n, i in enumerate(grp):
            _to_slabs(jnp.where(live, out[n * Q:(n + 1) * Q], 0.0), slab_scr.at[i])
            for r in range(Q // SUBLANES):
                start = (r % per_seg) * SUBLANES * SUBLANES + r // per_seg
                for s in range(slab_scr.shape[1]):
                    o_ref[i, r * SUBLANES:(r + 1) * SUBLANES, s * LANES:(s + 1) * LANES] = (
                        slab_scr[i, s, pl.ds(start, SUBLANES, stride=SUBLANES), :])

    groups = [tuple(range(g, g + REV_GROUP)) for g in range(0, SEQ_PER_STEP, REV_GROUP)]
    h = head(groups[0])
    mix = body(groups[0], h)
    for g in range(1, len(groups)):
        h_next = head(groups[g])
        tail(groups[g - 1], h, mix)
        h = h_next
        mix = body(groups[g], h)
    tail(groups[-1], h, mix)


def _rev(h, pk, sb, aux, lw):
    bsz, lp, d = h.shape
    nc = lp // Q
    w = LRU_WIDTH
    nb = SEQ_PER_STEP
    seq = lambda n: pl.BlockSpec((nb, Q, n), lambda i, c: (i, nc - 1 - c, 0))
    return pl.pallas_call(
        _rev_kernel,
        grid=(bsz // nb, nc),
        in_specs=[
            seq(d), seq(PK_WIDTH),
            pl.BlockSpec((nb, None, SSD_STATE, SSD_INNER), lambda i, c: (i, nc - 1 - c, 0, 0)),
            seq(LANES),
            _const_spec((d, SSD_INNER + w + 2 * d)),
            _const_spec((1, SSD_INNER)), _const_spec((1, SSD_INNER)), _const_spec((SSD_INNER, d)),
            _const_spec((w // 256, 256, 512)), _const_spec((1, w)), _const_spec((1, w)), _const_spec((1, w)),
            _const_spec((w, d)), _const_spec((d, d)), _const_spec((1, d)), _const_spec((1, d)),
        ],
        out_specs=seq(d),
        out_shape=jax.ShapeDtypeStruct((bsz, lp, d), F32),
        scratch_shapes=[pltpu.VMEM((nb, SSD_STATE, SSD_INNER), F32),
                        pltpu.VMEM((nb, d // LANES, SUBLANES * REV_PITCH, LANES), F32),
                        pltpu.VMEM((nb, Q, w), F32), pltpu.VMEM((nb, Q, w), F32), pltpu.VMEM((nb, 1, w), F32)],
        compiler_params=_cparams("parallel", "arbitrary"),
        name="rev_sweep",
    )(h, pk, sb, aux, lw["w_rev"], lw["dskip"], lw["norm_g"], lw["w_ssd_out"],
      lw["wg"][1], lw["brg"][1], lw["big"][1], lw["lam"][1], lw["w_lru_out"], lw["w_o"],
      lw["ln1_g"], lw["ln1_b"])


def _ffn_kernel(hp_ref, hc_ref, hn_ref, wu_ref, cw_ref, cb_ref, wd_ref, g_ref, b_ref, o_ref,
                ext_scr, out_scr):
    c = pl.program_id(1)
    nc = pl.num_programs(1)
    tf = hc_ref.shape[0]
    nseg = tf // SEG
    slabs = D_MODEL // LANES
    keep_next = jnp.where(c < nc - 1, 1.0, 0.0)
    for s in range(slabs):
        ls = slice(s * LANES, (s + 1) * LANES)
        ext_scr[s, 0:HALO, :] = hp_ref[:, ls]
        ext_scr[s, HALO:HALO + tf, :] = hc_ref[:, ls]
        ext_scr[s, HALO + tf:, :] = hn_ref[:, ls] * keep_next

    def group(g):
        halves = [jnp.concatenate([ext_scr[s, pl.ds(HALO + g + k * SUBLANES * nseg, SUBLANES, stride=nseg), :]
                                   for s in range(slabs)], axis=1) for k in range(SEG // SUBLANES)]
        return jnp.concatenate(halves, axis=0)
    xperm = jnp.concatenate([group(g) for g in range(-1, nseg + 1)], axis=0)
    xb = xperm.astype(BF16)

    def conv3(up, col):
        cw = cw_ref[:, pl.ds(col, FFN_COLS)].astype(BF16)
        acc = cb_ref[:, pl.ds(col, FFN_COLS)].astype(BF16) + up[0:tf] * cw[0:1, :]
        acc = acc + up[SEG:SEG + tf] * cw[1:2, :]
        return acc + up[2 * SEG:2 * SEG + tf] * cw[2:3, :]

    f = None
    for j in range(D_FF // FFN_COLS):
        cg, cv = j * FFN_COLS, D_FF + j * FFN_COLS
        ug = conv3(jnp.dot(xb, wu_ref[:, pl.ds(cg, FFN_COLS)], preferred_element_type=F32).astype(BF16), cg)
        uv = conv3(jnp.dot(xb, wu_ref[:, pl.ds(cv, FFN_COLS)], preferred_element_type=F32).astype(BF16), cv)
        part = jnp.dot(_gelu(ug) * uv, wd_ref[pl.ds(cg, FFN_COLS), :], preferred_element_type=F32)
        f = part if f is None else f + part
    y = _ln(ALPHA * xperm[SEG:SEG + tf] + f, g_ref[...], b_ref[...])
    p = _row_iota(tf)
    t = c * tf + (p & (SEG - 1)) * nseg + (p >> SEG_SHIFT)
    y = jnp.where(t >= PAD, y, 0.0)
    for j in range(nseg):
        for k in range(SEG // SUBLANES):
            r0 = j * SEG + k * SUBLANES
            for s in range(slabs):
                out_scr[s, pl.ds(j + k * SUBLANES * nseg, SUBLANES, stride=nseg), :] = (
                    y[r0:r0 + SUBLANES, s * LANES:(s + 1) * LANES])
    for s in range(slabs):
        o_ref[:, s * LANES:(s + 1) * LANES] = out_scr[s]


def _ffn_tile(lp):
    for parts in (4, 8, 2, 1):
        if lp % (parts * SEG) == 0 and lp // parts <= 640:
            return lp // parts
    return Q


def _ffn(h, lw):
    bsz, lp, d = h.shape
    tf = _ffn_tile(lp)
    nh = lp // HALO
    per = tf // HALO
    seq = pl.BlockSpec((None, tf, d), lambda i, c: (i, c, 0))
    return pl.pallas_call(
        _ffn_kernel,
        grid=(bsz, lp // tf),
        in_specs=[
            pl.BlockSpec((None, HALO, d), lambda i, c: (i, jnp.maximum(c * per - 1, 0), 0)),
            seq,
            pl.BlockSpec((None, HALO, d), lambda i, c: (i, jnp.minimum((c + 1) * per, nh - 1), 0)),
            _const_spec((d, 2 * D_FF)), _const_spec((3, 2 * D_FF)), _const_spec((1, 2 * D_FF)),
            _const_spec((D_FF, d)), _const_spec((1, d)), _const_spec((1, d)),
        ],
        out_specs=seq,
        out_shape=jax.ShapeDtypeStruct((bsz, lp, d), F32),
        scratch_shapes=[pltpu.VMEM((d // LANES, tf + 2 * HALO, LANES), F32),
                        pltpu.VMEM((d // LANES, tf, LANES), F32)],
        compiler_params=_cparams("parallel", "parallel"),
        name="ffn_ln",
    )(h, h, h, lw["w_up"], lw["ffn_conv_w"], lw["ffn_conv_b"], lw["w_down"], lw["ln2_g"], lw["ln2_b"])


def _pad_lanes(v):
    return jnp.pad(v.reshape(1, -1).astype(F32), ((0, 0), (0, LANES - v.size)))


def _gate_tiles(w_rg, w_ig):
    def tiles(w):
        w4 = w.reshape(LRU_HEADS // 4, 4, LRU_BLOCK, LRU_BLOCK)
        eye = jnp.eye(4, dtype=w.dtype)
        return jnp.einsum("thij,hk->thikj", w4, eye).reshape(LRU_HEADS // 4, 256, 256)
    return jnp.concatenate([tiles(w_rg), tiles(w_ig)], axis=-1).astype(BF16)


def _prep_layer(p, li):
    w_in = p["w_in"][li]
    row = lambda v: v.reshape(1, -1).astype(F32)
    w_dt = jnp.pad(w_in[:, S2:S3], ((0, 0), (0, LANES - (S3 - S2))))
    return dict(
        w_fwd=jnp.concatenate([w_in[:, S1:S2], w_in[:, S4:S5], w_dt], axis=1).astype(BF16),
        w_rev=jnp.concatenate([w_in[:, :S1], w_in[:, S3:S4], w_in[:, S5:]], axis=1).astype(BF16),
        ssd_conv_w=p["ssd_conv_w"][li], ssd_conv_b=row(p["ssd_conv_b"][li]),
        dt_bias=_pad_lanes(p["ssd_dt_bias"][li]), a_log=_pad_lanes(p["ssd_a_log"][li]),
        dskip=row(jnp.repeat(p["ssd_d"][li], SSD_HEAD_DIM)), norm_g=row(p["ssd_norm_g"][li]),
        w_ssd_out=p["w_ssd_out"][li].astype(BF16),
        lru_conv_w=p["lru_conv_w"][li], lru_conv_b=row(p["lru_conv_b"][li]),
        wg=[_gate_tiles(p["lru_w_rg"][li, d], p["lru_w_ig"][li, d]) for d in range(2)],
        brg=[row(p["lru_b_rg"][li, d]) for d in range(2)],
        big=[row(p["lru_b_ig"][li, d]) for d in range(2)],
        lam=[row(p["lru_lambda"][li, d]) for d in range(2)],
        w_lru_out=p["w_lru_out"][li].astype(BF16), w_o=p["w_o"][li].astype(BF16),
        ln1_g=row(p["ln1_g"][li]), ln1_b=row(p["ln1_b"][li]),
        w_up=p["w_up"][li].astype(BF16), ffn_conv_w=p["ffn_conv_w"][li],
        ffn_conv_b=row(p["ffn_conv_b"][li]), w_down=p["w_down"][li].astype(BF16),
        ln2_g=row(p["ln2_g"][li]), ln2_b=row(p["ln2_b"][li]),
    )


def _layer(h, lw):
    pk, sb, aux = _fwd(h, lw)
    h = _rev(h, pk, sb, aux, lw)
    return _ffn(h, lw)


def _run(x, meta_pad, ln_g, ln_b, layers):
    h = _embed(x, meta_pad, ln_g, ln_b)
    for lw in layers:
        h = _layer(h, lw)
    return h[:, Q:]


def kernel(x_prompt, x_sample, meta, ln_in_g, ln_in_b, w_in, ssd_conv_w, ssd_conv_b, ssd_dt_bias, ssd_a_log, ssd_d, ssd_norm_g, w_ssd_out, lru_conv_w, lru_conv_b, lru_w_rg, lru_b_rg, lru_w_ig, lru_b_ig, lru_lambda, w_lru_out, w_o, ln1_g, ln1_b, w_up, ffn_conv_w, ffn_conv_b, w_down, ln2_g, ln2_b):
    p = dict(w_in=w_in, ssd_conv_w=ssd_conv_w, ssd_conv_b=ssd_conv_b, ssd_dt_bias=ssd_dt_bias,
             ssd_a_log=ssd_a_log, ssd_d=ssd_d, ssd_norm_g=ssd_norm_g, w_ssd_out=w_ssd_out,
             lru_conv_w=lru_conv_w, lru_conv_b=lru_conv_b, lru_w_rg=lru_w_rg, lru_b_rg=lru_b_rg,
             lru_w_ig=lru_w_ig, lru_b_ig=lru_b_ig, lru_lambda=lru_lambda, w_lru_out=w_lru_out,
             w_o=w_o, ln1_g=ln1_g, ln1_b=ln1_b, w_up=w_up, ffn_conv_w=ffn_conv_w,
             ffn_conv_b=ffn_conv_b, w_down=w_down, ln2_g=ln2_g, ln2_b=ln2_b)
    layers = [_prep_layer(p, li) for li in range(w_in.shape[0])]
    meta_pad = jnp.pad(meta.astype(F32), ((PAD, 0), (0, 0)))
    g = ln_in_g.reshape(1, -1)
    b = ln_in_b.reshape(1, -1)
    return (_run(x_prompt, meta_pad, g, b, layers), _run(x_sample, meta_pad, g, b, layers))
```
